```python
import jax, jax.numpy as jnp
from jax import lax
import numpy as np

D_MODEL = 1024
BATCH = 4
SEQ = 8192
DEPTH = 2

N_MIXERS = 2
HEAD_DIM = 64
N_SLOTS = 8
DILATED_GROUPS = ((128, 1), (512, 4), (2048, 16))
N_GROUPS = 3
GROUP_WIDTH = N_SLOTS * HEAD_DIM
ROT_DIM = HEAD_DIM // 4
ROPE_THETA = 500000.0
CONV_KERNEL = 31
CONV_INNER = D_MODEL
FFN_DIM = 2816
FFN_CONV = 3
EPS = 1e-6

kernel_name = "hybrid_dilated_attn_conformer_convffn"


def _rmsnorm(x, g):
    xf = x.astype(jnp.float32)
    y = xf * lax.rsqrt(jnp.mean(xf * xf, axis=-1, keepdims=True) + EPS)
    return (y * g.astype(jnp.float32)).astype(x.dtype)


def _layernorm(x, g, b):
    xf = x.astype(jnp.float32)
    mu = jnp.mean(xf, axis=-1, keepdims=True)
    var = jnp.mean(jnp.square(xf - mu), axis=-1, keepdims=True)
    y = (xf - mu) * lax.rsqrt(var + EPS)
    return (y * g.astype(jnp.float32) + b.astype(jnp.float32)).astype(x.dtype)


def _causal_depthwise_conv(x, w):
    k, c = w.shape
    return lax.conv_general_dilated(
        x, w[:, None, :].astype(x.dtype), window_strides=(1,),
        padding=[(k - 1, 0)], dimension_numbers=("NWC", "WIO", "NWC"),
        feature_group_count=c)


def _partial_rope(t, positions):
    half = ROT_DIM // 2
    inv_freq = ROPE_THETA ** (-jnp.arange(half, dtype=jnp.float32) / half)
    ang = positions.astype(jnp.float32)[:, :, None] * inv_freq
    cos = jnp.cos(ang)[:, :, None, None, :]
    sin = jnp.sin(ang)[:, :, None, None, :]
    tr = t[..., :ROT_DIM].astype(jnp.float32)
    t1, t2 = tr[..., :half], tr[..., half:]
    rot = jnp.concatenate([t1 * cos - t2 * sin, t2 * cos + t1 * sin], axis=-1)
    return jnp.concatenate([rot.astype(t.dtype), t[..., ROT_DIM:]], axis=-1)


def _banded_causal_attention(q, k, v, span):
    n, h, l, dh = q.shape
    nb = l // span
    qb = q.reshape(n, h, nb, span, dh)
    kb = k.reshape(n, h, nb, span, dh)
    vb = v.reshape(n, h, nb, span, dh)

    def with_prev(t):
        prev = jnp.pad(t, ((0, 0), (0, 0), (1, 0), (0, 0), (0, 0)))[:, :, :-1]
        return jnp.concatenate([prev, t], axis=3)

    kc, vc = with_prev(kb), with_prev(vb)
    s = jnp.einsum("nhbqd,nhbkd->nhbqk", qb, kc,
                   preferred_element_type=jnp.float32) * (HEAD_DIM ** -0.5)
    qi = jnp.arange(span)[:, None] + span
    ki = jnp.arange(2 * span)[None, :]
    dist = qi - ki
    band = (dist >= 0) & (dist <= span)
    has_prev = (jnp.arange(nb)[:, None, None] > 0) | (ki[None] >= span)
    mask = band[None] & has_prev
    s = jnp.where(mask, s, -jnp.inf)
    m = jnp.max(s, axis=-1, keepdims=True)
    p = jnp.exp(s - m)
    denom = jnp.sum(p, axis=-1)
    o = jnp.einsum("nhbqk,nhbkd->nhbqd", p, vc.astype(jnp.float32)) / denom[..., None]
    lse = m[..., 0] + jnp.log(denom)
    return o.reshape(n, h, l, dh), lse.reshape(n, h, l)


def _dilated_group_attention(q, k, v, window, dilation):
    b, s, h, dh = q.shape
    span = window // dilation
    l = s // dilation
    lp = -(-l // span) * span

    def to_phases(t):
        t = t.reshape(b, l, dilation, h, dh).transpose(0, 2, 3, 1, 4)
        t = t.reshape(b * dilation, h, l, dh)
        return jnp.pad(t, ((0, 0), (0, 0), (0, lp - l), (0, 0)))

    o, lse = _banded_causal_attention(to_phases(q), to_phases(k), to_phases(v), span)
    o = o[:, :, :l].reshape(b, dilation, h, l, dh).transpose(0, 3, 1, 2, 4)
    lse = lse[:, :, :l].reshape(b, dilation, h, l).transpose(0, 3, 1, 2)
    return o.reshape(b, s, h, dh), lse.reshape(b, s, h)


def _dilated_attention(h, positions, w_qkv, w_o):
    b, s, _ = h.shape
    qkv = (h @ w_qkv).reshape(b, s, 3, N_GROUPS, N_SLOTS, HEAD_DIM)
    q = _partial_rope(qkv[:, :, 0], positions)
    k = _partial_rope(qkv[:, :, 1], positions)
    v = qkv[:, :, 2]
    outs, lses = [], []
    for g, (window, dilation) in enumerate(DILATED_GROUPS):
        o_g, lse_g = _dilated_group_attention(q[:, :, g], k[:, :, g], v[:, :, g],
                                              window, dilation)
        outs.append(o_g)
        lses.append(lse_g)
    o = jnp.stack(outs, axis=0)
    wgt = jax.nn.softmax(jnp.stack(lses, axis=0), axis=0)
    mixed = jnp.sum(wgt[..., None] * o, axis=0).astype(h.dtype)
    return mixed.reshape(b, s, GROUP_WIDTH) @ w_o


def _conformer_conv(h, w_pw1, b_pw1, w_dw, b_dw, ln_g, ln_b, w_pw2, b_pw2):
    a, gate = jnp.split(h @ w_pw1 + b_pw1, 2, axis=-1)
    u = a * jax.nn.sigmoid(gate)
    u = _causal_depthwise_conv(u, w_dw) + b_dw
    u = jax.nn.silu(_layernorm(u, ln_g, ln_b))
    return u @ w_pw2 + b_pw2


def _conv_ffn(h, w_up, w_dw, b_dw, w_down):
    ug = _causal_depthwise_conv(h @ w_up, w_dw) + b_dw
    up, gate = jnp.split(ug, 2, axis=-1)
    return (jax.nn.silu(gate) * up) @ w_down


def setup_inputs(seed: int = 0) -> dict:
    key = jax.random.key(seed)
    ks = jax.random.split(key, 20)
    n_attn = (DEPTH + N_MIXERS - 1) // N_MIXERS
    n_conv = DEPTH // N_MIXERS
    f32 = jnp.float32
    nrm = lambda k, shape, fan: jax.random.normal(k, shape, f32) * (fan ** -0.5)
    small = lambda k, shape: 0.02 * jax.random.normal(k, shape, f32)
    x = jax.random.normal(ks[0], (BATCH, SEQ, D_MODEL), f32)
    offset = jax.random.randint(ks[1], (BATCH, 1), 0, 4096, dtype=jnp.int32)
    positions = (jnp.arange(SEQ, dtype=jnp.int32)[None, :] + offset).astype(jnp.int32)
    return {
        "x": x,
        "positions": positions,
        "norm_g": 1.0 + small(ks[2], (DEPTH, 4, D_MODEL)),
        "attn_w_qkv": nrm(ks[3], (n_attn, D_MODEL, 3 * N_GROUPS * GROUP_WIDTH), D_MODEL),
        "attn_w_o": nrm(ks[4], (n_attn, GROUP_WIDTH, D_MODEL), GROUP_WIDTH),
        "conv_w_pw1": nrm(ks[5], (n_conv, D_MODEL, 2 * CONV_INNER), D_MODEL),
        "conv_b_pw1": small(ks[6], (n_conv, 2 * CONV_INNER)),
        "conv_w_dw": nrm(ks[7], (n_conv, CONV_KERNEL, CONV_INNER), CONV_KERNEL),
        "conv_b_dw": small(ks[8], (n_conv, CONV_INNER)),
        "conv_ln_g": 1.0 + small(ks[9], (n_conv, CONV_INNER)),
        "conv_ln_b": small(ks[10], (n_conv, CONV_INNER)),
        "conv_w_pw2": nrm(ks[11], (n_conv, CONV_INNER, D_MODEL), CONV_INNER),
        "conv_b_pw2": small(ks[12], (n_conv, D_MODEL)),
        "ffn_w_up": nrm(ks[13], (DEPTH, D_MODEL, 2 * FFN_DIM), D_MODEL),
        "ffn_w_dw": nrm(ks[14], (DEPTH, FFN_CONV, 2 * FFN_DIM), FFN_CONV),
        "ffn_b_dw": small(ks[15], (DEPTH, 2 * FFN_DIM)),
        "ffn_w_down": nrm(ks[16], (DEPTH, FFN_DIM, D_MODEL), FFN_DIM),
    }


def reference(x, positions, norm_g, attn_w_qkv, attn_w_o, conv_w_pw1, conv_b_pw1,
              conv_w_dw, conv_b_dw, conv_ln_g, conv_ln_b, conv_w_pw2, conv_b_pw2,
              ffn_w_up, ffn_w_dw, ffn_b_dw, ffn_w_down):
    for i in range(DEPTH):
        g = norm_g[i]
        j = i // N_MIXERS
        hn = _rmsnorm(x, g[0])
        if i % N_MIXERS == 0:
            y = _dilated_attention(hn, positions, attn_w_qkv[j], attn_w_o[j])
        else:
            y = _conformer_conv(hn, conv_w_pw1[j], conv_b_pw1[j], conv_w_dw[j],
                                conv_b_dw[j], conv_ln_g[j], conv_ln_b[j],
                                conv_w_pw2[j], conv_b_pw2[j])
        x = x + _rmsnorm(y, g[1])
        hn = _rmsnorm(x, g[2])
        y = _conv_ffn(hn, ffn_w_up[i], ffn_w_dw[i], ffn_b_dw[i], ffn_w_down[i])
        x = x + _rmsnorm(y, g[3])
    return x
```

```python
import functools

import jax
import jax.numpy as jnp
from jax import lax
from jax.experimental import pallas as pl
from jax.experimental.pallas import tpu as pltpu

D_MODEL = 1024
HEAD_DIM = 64
N_SLOTS = 8
DILATED_GROUPS = ((128, 1), (512, 4), (2048, 16))
N_GROUPS = 3
GROUP_WIDTH = N_SLOTS * HEAD_DIM
QKV_WIDTH = 3 * N_GROUPS * GROUP_WIDTH
ROT_DIM = HEAD_DIM // 4
ROPE_THETA = 500000.0
CONV_KERNEL = 31
FFN_DIM = 2816
FFN_CONV = 3
EPS = 1e-6

V7X_LANES = 128
V7X_SUBLANES = 8
V7X_VMEM_LIMIT_BYTES = 56 * 1024 * 1024

SPAN = 128
ROW_TILE = 512
ATT_TILE = 512
FFN_CHUNK = 256
FFN_NCHUNK = FFN_DIM // FFN_CHUNK
CONV_HALO = 32
CONV_ROWS = 64
NEG_BIG = -1e30

F32 = jnp.float32
BF16 = jnp.bfloat16


def _params(n_axes):
    return pltpu.CompilerParams(
        dimension_semantics=("arbitrary",) * n_axes,
        vmem_limit_bytes=V7X_VMEM_LIMIT_BYTES)


def _rms(x, g):
    return x * lax.rsqrt(jnp.mean(x * x, axis=-1, keepdims=True) + EPS) * g


def _sigmoid(x):
    return 1.0 / (1.0 + jnp.exp(-x))


def _qkv_kernel(x_ref, pos_ref, g_ref, invf_ref, sgnm_ref, sgnp_ref, w_ref,
                o_ref, hn_ref):
    hn_ref[...] = _rms(x_ref[...], g_ref[...]).astype(BF16)
    ang = pos_ref[...].astype(F32) * invf_ref[...]
    cos = jnp.cos(ang)
    sin = jnp.sin(ang)
    sin_m = sin * sgnm_ref[...]
    sin_p = sin * sgnp_ref[...]
    qscale = HEAD_DIM ** -0.5
    n_cb = QKV_WIDTH // GROUP_WIDTH
    for cb in range(n_cb):
        res = jnp.dot(hn_ref[...], w_ref[:, cb * GROUP_WIDTH:(cb + 1) * GROUP_WIDTH],
                      preferred_element_type=F32)
        if cb < 2 * N_GROUPS:
            scale = qscale if cb < N_GROUPS else 1.0
            for l in range(GROUP_WIDTH // V7X_LANES):
                t = res[:, l * V7X_LANES:(l + 1) * V7X_LANES]
                r = (t * cos
                     + pltpu.roll(t, V7X_LANES - ROT_DIM // 2, 1) * sin_m
                     + pltpu.roll(t, ROT_DIM // 2, 1) * sin_p)
                c0 = cb * GROUP_WIDTH + l * V7X_LANES
                o_ref[:, c0:c0 + V7X_LANES] = (r * scale).astype(BF16)
        else:
            o_ref[:, cb * GROUP_WIDTH:(cb + 1) * GROUP_WIDTH] = res.astype(BF16)


def _qkv_call(x, pos, g, invf, sgnm, sgnp, w):
    m = x.shape[0]
    tm = ROW_TILE
    row = lambda i: (i, 0)
    fixed = lambda i: (0, 0)
    return pl.pallas_call(
        _qkv_kernel,
        out_shape=jax.ShapeDtypeStruct((m, QKV_WIDTH), BF16),
        grid=(m // tm,),
        in_specs=[
            pl.BlockSpec((tm, D_MODEL), row),
            pl.BlockSpec((tm, 1), row),
            pl.BlockSpec((1, D_MODEL), fixed),
            pl.BlockSpec((1, V7X_LANES), fixed),
            pl.BlockSpec((1, V7X_LANES), fixed),
            pl.BlockSpec((1, V7X_LANES), fixed),
            pl.BlockSpec((D_MODEL, QKV_WIDTH), fixed),
        ],
        out_specs=pl.BlockSpec((tm, QKV_WIDTH), row),
        scratch_shapes=[pltpu.VMEM((tm, D_MODEL), BF16)],
        compiler_params=_params(1),
        name="qkv_rope",
    )(x, pos, g, invf, sgnm, sgnp, w)


def _attn_kernel(q_ref, kc_ref, vc_ref, kp_ref, vp_ref, o_ref, lse_ref):
    jb = pl.program_id(1)
    lane = lax.broadcasted_iota(jnp.int32, (SPAN, V7X_LANES), 1)
    row = lax.broadcasted_iota(jnp.int32, (2 * SPAN, 2 * SPAN), 0) & (SPAN - 1)
    col = lax.broadcasted_iota(jnp.int32, (2 * SPAN, 2 * SPAN), 1)
    band = (col >= row) & (col <= row + SPAN)
    first = jnp.logical_and(band, jnp.logical_or(col >= SPAN, jb > 0))
    lo = lane < HEAD_DIM
    n_sub = ATT_TILE // SPAN
    for i in range(n_sub):
        mask = first if i == 0 else band
        rows = slice(i * SPAN, (i + 1) * SPAN)
        lse_acc = jnp.zeros((SPAN, V7X_LANES), F32)
        for p in range(GROUP_WIDTH // V7X_LANES):
            cs = slice(p * V7X_LANES, (p + 1) * V7X_LANES)
            q2 = q_ref[rows, cs]
            if i == 0:
                kcat = jnp.concatenate([kp_ref[:, cs], kc_ref[0:SPAN, cs]], axis=0)
                vcat = jnp.concatenate([vp_ref[:, cs], vc_ref[0:SPAN, cs]], axis=0)
            else:
                kcat = kc_ref[(i - 1) * SPAN:(i + 1) * SPAN, cs]
                vcat = vc_ref[(i - 1) * SPAN:(i + 1) * SPAN, cs]
            zero = jnp.zeros_like(q2)
            qs = jnp.concatenate([jnp.where(lo, q2, zero), jnp.where(lo, zero, q2)],
                                 axis=0)
            s = lax.dot_general(qs, kcat, (((1,), (1,)), ((), ())),
                                preferred_element_type=F32)
            s = jnp.where(mask, s, NEG_BIG)
            mx = jnp.max(s, axis=-1, keepdims=True)
            pr = jnp.exp(s - mx)
            den = jnp.sum(pr, axis=-1, keepdims=True)
            pv = jnp.dot(pr.astype(BF16), vcat, preferred_element_type=F32)
            rinv = 1.0 / den
            o = jnp.where(lo, pv[:SPAN] * rinv[:SPAN], pv[SPAN:] * rinv[SPAN:])
            o_ref[rows, cs] = o.astype(BF16)
            lse = mx + jnp.log(den)
            lse_acc = jnp.where(lane == 2 * p, lse[:SPAN], lse_acc)
            lse_acc = jnp.where(lane == 2 * p + 1, lse[SPAN:], lse_acc)
        lse_ref[rows, :] = lse_acc


def _attn_call(qkv, batch, seq, group, dilation):
    l = seq // dilation
    nq = QKV_WIDTH // GROUP_WIDTH
    view = qkv.reshape(batch, l, dilation * QKV_WIDTH)
    sub = ATT_TILE // SPAN

    def spec(rows, which, prev):
        def imap(b, j, r):
            jj = jnp.maximum(j * sub - 1, 0) if prev else j
            return (b, jj, r * nq + which * N_GROUPS + group)
        return pl.BlockSpec((None, rows, GROUP_WIDTH), imap)

    o, lse = pl.pallas_call(
        _attn_kernel,
        out_shape=(jax.ShapeDtypeStruct((batch, l, dilation * GROUP_WIDTH), BF16),
                   jax.ShapeDtypeStruct((batch, l, dilation * V7X_LANES), F32)),
        grid=(batch, l // ATT_TILE, dilation),
        in_specs=[spec(ATT_TILE, 0, False), spec(ATT_TILE, 1, False),
                  spec(ATT_TILE, 2, False), spec(SPAN, 1, True), spec(SPAN, 2, True)],
        out_specs=(pl.BlockSpec((None, ATT_TILE, GROUP_WIDTH), lambda b, j, r: (b, j, r)),
                   pl.BlockSpec((None, ATT_TILE, V7X_LANES), lambda b, j, r: (b, j, r))),
        compiler_params=_params(3),
        name=f"dilated_attn_d{dilation}",
    )(view, view, view, view, view)
    m = batch * seq
    return o.reshape(m, GROUP_WIDTH), lse.reshape(m, V7X_LANES)


def _merge_kernel(o1_ref, o2_ref, o3_ref, l1_ref, l2_ref, l3_ref, x_ref, wo_ref,
                  g_ref, expand_ref, out_ref):
    lses = [l1_ref[...], l2_ref[...], l3_ref[...]]
    mx = jnp.maximum(jnp.maximum(lses[0], lses[1]), lses[2])
    es = [jnp.exp(l - mx) for l in lses]
    rden = 1.0 / (es[0] + es[1] + es[2])
    mixed = None
    for e, o_ref in zip(es, (o1_ref, o2_ref, o3_ref)):
        w = e * rden
        hi = w.astype(BF16)
        lo = (w - hi.astype(F32)).astype(BF16)
        wb = (jnp.dot(hi, expand_ref[...], preferred_element_type=F32)
              + jnp.dot(lo, expand_ref[...], preferred_element_type=F32))
        term = wb * o_ref[...].astype(F32)
        mixed = term if mixed is None else mixed + term
    y = jnp.dot(mixed.astype(BF16), wo_ref[...], preferred_element_type=F32)
    out_ref[...] = x_ref[...] + _rms(y, g_ref[...])


def _merge_call(os, lses, x, wo, g, expand):
    m = x.shape[0]
    tm = ROW_TILE
    row = lambda i: (i, 0)
    fixed = lambda i: (0, 0)
    return pl.pallas_call(
        _merge_kernel,
        out_shape=jax.ShapeDtypeStruct((m, D_MODEL), F32),
        grid=(m // tm,),
        in_specs=[pl.BlockSpec((tm, GROUP_WIDTH), row)] * 3
        + [pl.BlockSpec((tm, V7X_LANES), row)] * 3
        + [pl.BlockSpec((tm, D_MODEL), row),
           pl.BlockSpec((GROUP_WIDTH, D_MODEL), fixed),
           pl.BlockSpec((1, D_MODEL), fixed),
           pl.BlockSpec((V7X_LANES, GROUP_WIDTH), fixed)],
        out_specs=pl.BlockSpec((tm, D_MODEL), row),
        compiler_params=_params(1),
        name="attn_merge_out",
    )(*os, *lses, x, wo, g, expand)


def _ffn_kernel(x_ref, g_in_ref, g_out_ref, wup_ref, wdw_ref, bdw_ref, wdn_ref,
                out_ref, hn_ref, s_ref, carry_ref, acc_ref, *, tiles_per_seq):
    tm = x_ref.shape[0]
    pad = V7X_SUBLANES

    @pl.when(pl.program_id(0) % tiles_per_seq == 0)
    def _():
        carry_ref[...] = jnp.zeros_like(carry_ref)

    hn_ref[...] = _rms(x_ref[...], g_in_ref[...]).astype(BF16)
    acc_ref[...] = jnp.zeros_like(acc_ref)

    def chunk(c, carry):
        ug = jnp.dot(hn_ref[...], wup_ref[c], preferred_element_type=F32)
        s_ref[0:pad, :] = carry_ref[c]
        s_ref[pad:pad + tm, :] = ug
        carry_ref[c] = s_ref[tm:tm + pad, :]
        w = wdw_ref[c]
        conv = (s_ref[pl.ds(pad, tm), :] * w[2:3]
                + s_ref[pl.ds(pad - 1, tm), :] * w[1:2]
                + s_ref[pl.ds(pad - 2, tm), :] * w[0:1]
                + bdw_ref[c])
        up = conv[:, :FFN_CHUNK]
        gate = conv[:, FFN_CHUNK:]
        act = (gate * _sigmoid(gate) * up).astype(BF16)
        acc_ref[...] += jnp.dot(act, wdn_ref[c], preferred_element_type=F32)
        return carry

    lax.fori_loop(0, FFN_NCHUNK, chunk, 0)
    out_ref[...] = x_ref[...] + _rms(acc_ref[...], g_out_ref[...])


def _ffn_call(x, g_in, g_out, wup, wdw, bdw, wdn, seq):
    m = x.shape[0]
    tm = ROW_TILE
    row = lambda i: (i, 0)
    fixed2 = lambda i: (0, 0)
    fixed3 = lambda i: (0, 0, 0)
    return pl.pallas_call(
        functools.partial(_ffn_kernel, tiles_per_seq=seq // tm),
        out_shape=jax.ShapeDtypeStruct((m, D_MODEL), F32),
        grid=(m // tm,),
        in_specs=[
            pl.BlockSpec((tm, D_MODEL), row),
            pl.BlockSpec((1, D_MODEL), fixed2),
            pl.BlockSpec((1, D_MODEL), fixed2),
            pl.BlockSpec((FFN_NCHUNK, D_MODEL, 2 * FFN_CHUNK), fixed3),
            pl.BlockSpec((FFN_NCHUNK, V7X_SUBLANES, 2 * FFN_CHUNK), fixed3),
            pl.BlockSpec((FFN_NCHUNK, 1, 2 * FFN_CHUNK), fixed3),
            pl.BlockSpec((FFN_NCHUNK, FFN_CHUNK, D_MODEL), fixed3),
        ],
        out_specs=pl.BlockSpec((tm, D_MODEL), row),
        scratch_shapes=[
            pltpu.VMEM((tm, D_MODEL), BF16),
            pltpu.VMEM((tm + V7X_SUBLANES, 2 * FFN_CHUNK), F32),
            pltpu.VMEM((FFN_NCHUNK, V7X_SUBLANES, 2 * FFN_CHUNK), F32),
            pltpu.VMEM((tm, D_MODEL), F32),
        ],
        compiler_params=_params(1),
        name="conv_ffn",
    )(x, g_in, g_out, wup, wdw, bdw, wdn)


def _conformer_kernel(x_ref, g_in_ref, g_out_ref, w1_ref, b1_ref, wdw_ref, bdw_ref,
                      lng_ref, lnb_ref, w2_ref, b2_ref, out_ref, s_ref, y_ref,
                      *, tiles_per_seq):
    tm = x_ref.shape[0]
    halo = CONV_HALO

    n_lb = D_MODEL // V7X_LANES

    @pl.when(pl.program_id(0) % tiles_per_seq == 0)
    def _():
        s_ref[:, 0:halo, :] = jnp.zeros((n_lb, halo, V7X_LANES), F32)

    x = x_ref[...]
    hn = _rms(x, g_in_ref[...]).astype(BF16)
    ag = jnp.dot(hn, w1_ref[...], preferred_element_type=F32) + b1_ref[...]
    u = ag[:, :D_MODEL] * _sigmoid(ag[:, D_MODEL:])
    for lb in range(n_lb):
        s_ref[lb, halo:halo + tm, :] = u[:, lb * V7X_LANES:(lb + 1) * V7X_LANES]

    first = halo - (CONV_KERNEL - 1)

    def lane_block(lb, carry):
        for rc in range(tm // CONV_ROWS):
            acc = jnp.zeros((CONV_ROWS, V7X_LANES), F32)
            for j in range(CONV_KERNEL):
                acc = acc + (s_ref[lb, pl.ds(rc * CONV_ROWS + first + j, CONV_ROWS), :]
                             * wdw_ref[lb, j:j + 1, :])
            y_ref[lb, pl.ds(rc * CONV_ROWS, CONV_ROWS), :] = acc
        return carry

    lax.fori_loop(0, n_lb, lane_block, 0)

    s_ref[:, 0:halo, :] = s_ref[:, tm:tm + halo, :]

    v = jnp.concatenate([y_ref[lb] for lb in range(n_lb)], axis=1) + bdw_ref[...]
    mu = jnp.mean(v, axis=-1, keepdims=True)
    vc = v - mu
    var = jnp.mean(vc * vc, axis=-1, keepdims=True)
    ln = vc * lax.rsqrt(var + EPS) * lng_ref[...] + lnb_ref[...]
    act = (ln * _sigmoid(ln)).astype(BF16)
    y = jnp.dot(act, w2_ref[...], preferred_element_type=F32) + b2_ref[...]
    out_ref[...] = x + _rms(y, g_out_ref[...])


def _conformer_call(x, g_in, g_out, w1, b1, wdw, bdw, lng, lnb, w2, b2, seq):
    m = x.shape[0]
    tm = ROW_TILE
    row = lambda i: (i, 0)
    fixed = lambda i: (0, 0)
    vec = pl.BlockSpec((1, D_MODEL), fixed)
    return pl.pallas_call(
        functools.partial(_conformer_kernel, tiles_per_seq=seq // tm),
        out_shape=jax.ShapeDtypeStruct((m, D_MODEL), F32),
        grid=(m // tm,),
        in_specs=[
            pl.BlockSpec((tm, D_MODEL), row), vec, vec,
            pl.BlockSpec((D_MODEL, 2 * D_MODEL), fixed),
            pl.BlockSpec((1, 2 * D_MODEL), fixed),
            pl.BlockSpec((D_MODEL // V7X_LANES, CONV_HALO, V7X_LANES), lambda i: (0, 0, 0)),
            vec, vec, vec,
            pl.BlockSpec((D_MODEL, D_MODEL), fixed),
            vec,
        ],
        out_specs=pl.BlockSpec((tm, D_MODEL), row),
        scratch_shapes=[
            pltpu.VMEM((D_MODEL // V7X_LANES, tm + CONV_HALO, V7X_LANES), F32),
            pltpu.VMEM((D_MODEL // V7X_LANES, tm, V7X_LANES), F32),
        ],
        compiler_params=_params(1),
        name="conformer_conv",
    )(x, g_in, g_out, w1, b1, wdw, bdw, lng, lnb, w2, b2)


def _rope_tables():
    half = ROT_DIM // 2
    inv_freq = ROPE_THETA ** (-jnp.arange(half, dtype=F32) / half)
    d = jnp.arange(V7X_LANES) % HEAD_DIM
    invf = jnp.where(d < ROT_DIM, jnp.tile(inv_freq, V7X_LANES // half), 0.0)
    sgnm = jnp.where(d < half, -1.0, 0.0)
    sgnp = jnp.where((d >= half) & (d < ROT_DIM), 1.0, 0.0)
    as_row = lambda a: a.astype(F32).reshape(1, V7X_LANES)
    return as_row(invf), as_row(sgnm), as_row(sgnp)


def _ffn_weights(w_up, w_dw, b_dw, w_down):
    n, tf = FFN_NCHUNK, FFN_CHUNK
    wup = (w_up.reshape(D_MODEL, 2, n, tf).transpose(2, 0, 1, 3)
           .reshape(n, D_MODEL, 2 * tf).astype(BF16))
    wdw = w_dw.reshape(FFN_CONV, 2, n, tf).transpose(2, 0, 1, 3).reshape(n, FFN_CONV, 2 * tf)
    wdw = jnp.pad(wdw, ((0, 0), (0, V7X_SUBLANES - FFN_CONV), (0, 0)))
    bdw = b_dw.reshape(2, n, tf).transpose(1, 0, 2).reshape(n, 1, 2 * tf)
    wdn = w_down.reshape(n, tf, D_MODEL).astype(BF16)
    return wup, wdw, bdw, wdn


def kernel(x, positions, norm_g, attn_w_qkv, attn_w_o, conv_w_pw1, conv_b_pw1, conv_w_dw, conv_b_dw, conv_ln_g, conv_ln_b, conv_w_pw2, conv_b_pw2, ffn_w_up, ffn_w_dw, ffn_b_dw, ffn_w_down):
    batch, seq, d_model = x.shape
    m = batch * seq
    xf = x.reshape(m, d_model)
    pos = positions.reshape(m, 1)
    vec = lambda a: a.reshape(1, -1)

    g = norm_g[0]
    invf, sgnm, sgnp = _rope_tables()
    qkv = _qkv_call(xf, pos, vec(g[0]), invf, sgnm, sgnp, attn_w_qkv[0].astype(BF16))
    os, lses = [], []
    for gi, (window, dilation) in enumerate(DILATED_GROUPS):
        assert window // dilation == SPAN
        o_g, lse_g = _attn_call(qkv, batch, seq, gi, dilation)
        os.append(o_g)
        lses.append(lse_g)
    slot = jnp.arange(V7X_LANES)[:, None]
    expand = (jnp.arange(GROUP_WIDTH)[None, :] // HEAD_DIM == slot).astype(BF16)
    xf = _merge_call(os, lses, xf, attn_w_o[0].astype(BF16), vec(g[1]), expand)
    xf = _ffn_call(xf, vec(g[2]), vec(g[3]),
                   *_ffn_weights(ffn_w_up[0], ffn_w_dw[0], ffn_b_dw[0], ffn_w_down[0]), seq)

    g = norm_g[1]
    wdw = jnp.pad(conv_w_dw[0], ((0, CONV_HALO - CONV_KERNEL), (0, 0)))
    wdw = wdw.reshape(CONV_HALO, D_MODEL // V7X_LANES, V7X_LANES).transpose(1, 0, 2)
    xf = _conformer_call(xf, vec(g[0]), vec(g[1]),
                         conv_w_pw1[0].astype(BF16), vec(conv_b_pw1[0]),
                         wdw, vec(conv_b_dw[0]), vec(conv_ln_g[0]), vec(conv_ln_b[0]),
                         conv_w_pw2[0].astype(BF16), vec(conv_b_pw2[0]), seq)
    xf = _ffn_call(xf, vec(g[2]), vec(g[3]),
                   *_ffn_weights(ffn_w_up[1], ffn_w_dw[1], ffn_b_dw[1], ffn_w_down[1]), seq)
    return xf.reshape(batch, seq, d_model)
```

```python
import functools

import jax
import jax.numpy as jnp
from jax import lax
from jax.experimental import pallas as pl
from jax.experimental.pallas import tpu as pltpu

D_MODEL = 1024
HEAD_DIM = 64
N_SLOTS = 8
DILATED_GROUPS = ((128, 1), (512, 4), (2048, 16))
N_GROUPS = 3
GROUP_WIDTH = N_SLOTS * HEAD_DIM
QKV_WIDTH = 3 * N_GROUPS * GROUP_WIDTH
ROT_DIM = HEAD_DIM // 4
ROPE_THETA = 500000.0
CONV_KERNEL = 31
FFN_DIM = 2816
FFN_CONV = 3
EPS = 1e-6

V7X_LANES = 128
V7X_SUBLANES = 8
V7X_VMEM_LIMIT_BYTES = 56 * 1024 * 1024

SPAN = 128
ROW_TILE = 512
ATT_TILE = 512
FFN_CHUNK = 256
FFN_NCHUNK = FFN_DIM // FFN_CHUNK
CONV_HALO = 32
CONV_ROWS = 64
NEG_BIG = -1e30
SLABS = GROUP_WIDTH // V7X_LANES
D_SLABS = D_MODEL // V7X_LANES

F32 = jnp.float32
BF16 = jnp.bfloat16


def _params(n_axes):
    return pltpu.CompilerParams(
        dimension_semantics=("arbitrary",) * n_axes,
        vmem_limit_bytes=V7X_VMEM_LIMIT_BYTES)


def _rms(x, g):
    return x * lax.rsqrt(jnp.mean(x * x, axis=-1, keepdims=True) + EPS) * g


def _sigmoid(x):
    return 1.0 / (1.0 + jnp.exp(-x))


def _slab(l):
    return slice(l * V7X_LANES, (l + 1) * V7X_LANES)


def _qkv_kernel(x_ref, pos_ref, g_ref, invf_ref, sgnm_ref, sgnp_ref, w_ref,
                o1_ref, o2_ref, o3_ref, hn_ref, perm_ref):
    tm = x_ref.shape[0]
    hn_ref[...] = _rms(x_ref[...], g_ref[...]).astype(BF16)
    ang = pos_ref[...].astype(F32) * invf_ref[...]
    cos = jnp.cos(ang)
    sin = jnp.sin(ang)
    sin_m = sin * sgnm_ref[...]
    sin_p = sin * sgnp_ref[...]
    qscale = HEAD_DIM ** -0.5
    outs = (o1_ref, o2_ref, o3_ref)
    for cb in range(QKV_WIDTH // GROUP_WIDTH):
        which, group = divmod(cb, N_GROUPS)
        dil = DILATED_GROUPS[group][1]
        o_ref = outs[group]
        res = jnp.dot(hn_ref[...], w_ref[:, cb * GROUP_WIDTH:(cb + 1) * GROUP_WIDTH],
                      preferred_element_type=F32)
        for l in range(SLABS):
            t = res[:, _slab(l)]
            if which < 2:
                t = (t * cos
                     + pltpu.roll(t, V7X_LANES - ROT_DIM // 2, 1) * sin_m
                     + pltpu.roll(t, ROT_DIM // 2, 1) * sin_p)
                if which == 0:
                    t = t * qscale
            cols = slice(which * GROUP_WIDTH + l * V7X_LANES,
                         which * GROUP_WIDTH + (l + 1) * V7X_LANES)
            if dil == 1:
                o_ref[0, :, cols] = t.astype(BF16)
            else:
                slot = (cb * SLABS + l) % perm_ref.shape[0]
                perm_ref[slot] = t
                for r in range(dil):
                    o_ref[r, :, cols] = perm_ref[slot, pl.ds(r, tm // dil, stride=dil), :].astype(BF16)


def _qkv_call(x, pos, g, invf, sgnm, sgnp, w):
    batch, seq, _ = x.shape
    tm = ROW_TILE
    row = lambda b, i: (b, i, 0)
    fixed = lambda b, i: (0, 0)
    out_shape, out_specs = [], []
    for _, dil in DILATED_GROUPS:
        out_shape.append(jax.ShapeDtypeStruct((batch, dil, seq // dil, 3 * GROUP_WIDTH), BF16))
        out_specs.append(pl.BlockSpec((None, dil, tm // dil, 3 * GROUP_WIDTH),
                                      lambda b, i: (b, 0, i, 0)))
    return pl.pallas_call(
        _qkv_kernel,
        out_shape=tuple(out_shape),
        grid=(batch, seq // tm),
        in_specs=[
            pl.BlockSpec((None, tm, D_MODEL), row),
            pl.BlockSpec((None, tm, 1), row),
            pl.BlockSpec((1, D_MODEL), fixed),
            pl.BlockSpec((1, V7X_LANES), fixed),
            pl.BlockSpec((1, V7X_LANES), fixed),
            pl.BlockSpec((1, V7X_LANES), fixed),
            pl.BlockSpec((D_MODEL, QKV_WIDTH), fixed),
        ],
        out_specs=tuple(out_specs),
        scratch_shapes=[pltpu.VMEM((tm, D_MODEL), BF16),
                        pltpu.VMEM((2 * SLABS, tm, V7X_LANES), F32)],
        compiler_params=_params(2),
        name="qkv_rope",
    )(x, pos, g, invf, sgnm, sgnp, w)


def _attn_kernel(q_ref, kc_ref, vc_ref, kp_ref, vp_ref, o_ref, lse_ref):
    jb = pl.program_id(2)
    lane = lax.broadcasted_iota(jnp.int32, (SPAN, V7X_LANES), 1)
    row = lax.broadcasted_iota(jnp.int32, (2 * SPAN, 2 * SPAN), 0) & (SPAN - 1)
    col = lax.broadcasted_iota(jnp.int32, (2 * SPAN, 2 * SPAN), 1)
    band = (col >= row) & (col <= row + SPAN)
    first = jnp.logical_and(band, jnp.logical_or(col >= SPAN, jb > 0))
    lo = lane < HEAD_DIM
    for i in range(ATT_TILE // SPAN):
        mask = first if i == 0 else band
        rows = slice(i * SPAN, (i + 1) * SPAN)
        lse_acc = jnp.zeros((SPAN, V7X_LANES), F32)
        for p in range(SLABS):
            cs = _slab(p)
            q2 = q_ref[rows, cs]
            if i == 0:
                kcat = jnp.concatenate([kp_ref[:, cs], kc_ref[0:SPAN, cs]], axis=0)
                vcat = jnp.concatenate([vp_ref[:, cs], vc_ref[0:SPAN, cs]], axis=0)
            else:
                kcat = kc_ref[(i - 1) * SPAN:(i + 1) * SPAN, cs]
                vcat = vc_ref[(i - 1) * SPAN:(i + 1) * SPAN, cs]
            zero = jnp.zeros_like(q2)
            qs = jnp.concatenate([jnp.where(lo, q2, zero), jnp.where(lo, zero, q2)],
                                 axis=0)
            s = lax.dot_general(qs, kcat, (((1,), (1,)), ((), ())),
                                preferred_element_type=F32)
            s = jnp.where(mask, s, NEG_BIG)
            mx = jnp.max(s, axis=-1, keepdims=True)
            pr = jnp.exp(s - mx)
            den = jnp.sum(pr, axis=-1, keepdims=True)
            pv = jnp.dot(pr.astype(BF16), vcat, preferred_element_type=F32)
            rinv = 1.0 / den
            o = jnp.where(lo, pv[:SPAN] * rinv[:SPAN], pv[SPAN:] * rinv[SPAN:])
            o_ref[rows, cs] = o.astype(BF16)
            lse = mx + jnp.log(den)
            lse_acc = jnp.where(lane == 2 * p, lse[:SPAN], lse_acc)
            lse_acc = jnp.where(lane == 2 * p + 1, lse[SPAN:], lse_acc)
        lse_ref[rows, :] = lse_acc


def _attn_call(qkv_g):
    batch, dil, l, _ = qkv_g.shape
    sub = ATT_TILE // SPAN

    def spec(rows, which, prev):
        def imap(b, r, j):
            return (b, r, jnp.maximum(j * sub - 1, 0) if prev else j, which)
        return pl.BlockSpec((None, None, rows, GROUP_WIDTH), imap)

    here = lambda b, r, j: (b, r, j, 0)
    return pl.pallas_call(
        _attn_kernel,
        out_shape=(jax.ShapeDtypeStruct((batch, dil, l, GROUP_WIDTH), BF16),
                   jax.ShapeDtypeStruct((batch, dil, l, V7X_LANES), F32)),
        grid=(batch, dil, l // ATT_TILE),
        in_specs=[spec(ATT_TILE, 0, False), spec(ATT_TILE, 1, False),
                  spec(ATT_TILE, 2, False), spec(SPAN, 1, True), spec(SPAN, 2, True)],
        out_specs=(pl.BlockSpec((None, None, ATT_TILE, GROUP_WIDTH), here),
                   pl.BlockSpec((None, None, ATT_TILE, V7X_LANES), here)),
        compiler_params=_params(3),
        name=f"dilated_attn_d{dil}",
    )(qkv_g, qkv_g, qkv_g, qkv_g, qkv_g)


def _merge_kernel(o1_ref, o2_ref, o3_ref, l1_ref, l2_ref, l3_ref, x_ref, wo_ref,
                  g_ref, expand_ref, out_ref, os_ref, ls_ref):
    tm = x_ref.shape[0]
    o_tok, lses = [], []
    for gi, (o_ref, l_ref) in enumerate(((o1_ref, l1_ref), (o2_ref, l2_ref),
                                         (o3_ref, l3_ref))):
        dil = o_ref.shape[0]
        if dil == 1:
            o_tok.append(o_ref[0].astype(F32))
            lses.append(l_ref[0])
            continue
        for r in range(dil):
            rows = pl.ds(r, tm // dil, stride=dil)
            o_r = o_ref[r].astype(F32)
            for l in range(SLABS):
                os_ref[gi - 1, l, rows, :] = o_r[:, _slab(l)]
            ls_ref[gi - 1, rows, :] = l_ref[r]
        o_tok.append(jnp.concatenate([os_ref[gi - 1, l] for l in range(SLABS)], axis=1))
        lses.append(ls_ref[gi - 1])
    mx = jnp.maximum(jnp.maximum(lses[0], lses[1]), lses[2])
    es = [jnp.exp(l - mx) for l in lses]
    rden = 1.0 / (es[0] + es[1] + es[2])
    mixed = None
    for e, o in zip(es, o_tok):
        w = e * rden
        hi = w.astype(BF16)
        lo = (w - hi.astype(F32)).astype(BF16)
        wb = (jnp.dot(hi, expand_ref[...], preferred_element_type=F32)
              + jnp.dot(lo, expand_ref[...], preferred_element_type=F32))
        term = wb * o
        mixed = term if mixed is None else mixed + term
    y = jnp.dot(mixed.astype(BF16), wo_ref[...], preferred_element_type=F32)
    out_ref[...] = x_ref[...] + _rms(y, g_ref[...])


def _merge_call(os, lses, x, wo, g, expand):
    batch, seq, _ = x.shape
    tm = ROW_TILE
    row = lambda b, i: (b, i, 0)
    fixed = lambda b, i: (0, 0)
    phase = lambda b, i: (b, 0, i, 0)
    o_specs = [pl.BlockSpec((None, d, tm // d, GROUP_WIDTH), phase) for _, d in DILATED_GROUPS]
    l_specs = [pl.BlockSpec((None, d, tm // d, V7X_LANES), phase) for _, d in DILATED_GROUPS]
    return pl.pallas_call(
        _merge_kernel,
        out_shape=jax.ShapeDtypeStruct((batch, seq, D_MODEL), F32),
        grid=(batch, seq // tm),
        in_specs=o_specs + l_specs
        + [pl.BlockSpec((None, tm, D_MODEL), row),
           pl.BlockSpec((GROUP_WIDTH, D_MODEL), fixed),
           pl.BlockSpec((1, D_MODEL), fixed),
           pl.BlockSpec((V7X_LANES, GROUP_WIDTH), fixed)],
        out_specs=pl.BlockSpec((None, tm, D_MODEL), row),
        scratch_shapes=[pltpu.VMEM((N_GROUPS - 1, SLABS, tm, V7X_LANES), F32),
                        pltpu.VMEM((N_GROUPS - 1, tm, V7X_LANES), F32)],
        compiler_params=_params(2),
        name="attn_merge_out",
    )(*os, *lses, x, wo, g, expand)


def _ffn_kernel(x_ref, g_in_ref, g_out_ref, wup_ref, wdw_ref, bdw_ref, wdn_ref,
                out_ref, hn_ref, s_ref, carry_ref, act_ref, *, tiles_per_seq):
    tm = x_ref.shape[0]
    pad = V7X_SUBLANES
    n_sl = 2 * FFN_CHUNK // V7X_LANES

    @pl.when(pl.program_id(0) % tiles_per_seq == 0)
    def _():
        carry_ref[...] = jnp.zeros_like(carry_ref)

    hn_ref[...] = _rms(x_ref[...], g_in_ref[...]).astype(BF16)

    for c in range(FFN_NCHUNK):
        ug = jnp.dot(hn_ref[...], wup_ref[c], preferred_element_type=F32)
        w = wdw_ref[c]
        b = bdw_ref[c]
        conv = []
        for l in range(n_sl):
            sl = s_ref.at[(c % 2) * n_sl + l]
            cur = ug[:, _slab(l)]
            sl[0:pad, :] = carry_ref[c, l]
            sl[pad:pad + tm, :] = cur
            carry_ref[c, l] = cur[tm - pad:tm, :]
            conv.append(cur * w[2:3, _slab(l)]
                        + sl[pl.ds(pad - 1, tm), :] * w[1:2, _slab(l)]
                        + sl[pl.ds(pad - 2, tm), :] * w[0:1, _slab(l)]
                        + b[:, _slab(l)])
        for l in range(n_sl // 2):
            gate = conv[n_sl // 2 + l]
            act = gate * _sigmoid(gate) * conv[l]
            act_ref[:, c * FFN_CHUNK + l * V7X_LANES:c * FFN_CHUNK + (l + 1) * V7X_LANES] = (
                act.astype(BF16))

    y = jnp.dot(act_ref[...], wdn_ref[...], preferred_element_type=F32)
    out_ref[...] = x_ref[...] + _rms(y, g_out_ref[...])


def _ffn_call(x, g_in, g_out, wup, wdw, bdw, wdn):
    batch, seq, _ = x.shape
    m = batch * seq
    tm = ROW_TILE
    n_sl = 2 * FFN_CHUNK // V7X_LANES
    row = lambda i: (i, 0)
    fixed2 = lambda i: (0, 0)
    fixed3 = lambda i: (0, 0, 0)
    out = pl.pallas_call(
        functools.partial(_ffn_kernel, tiles_per_seq=seq // tm),
        out_shape=jax.ShapeDtypeStruct((m, D_MODEL), F32),
        grid=(m // tm,),
        in_specs=[
            pl.BlockSpec((tm, D_MODEL), row),
            pl.BlockSpec((1, D_MODEL), fixed2),
            pl.BlockSpec((1, D_MODEL), fixed2),
            pl.BlockSpec((FFN_NCHUNK, D_MODEL, 2 * FFN_CHUNK), fixed3),
            pl.BlockSpec((FFN_NCHUNK, V7X_SUBLANES, 2 * FFN_CHUNK), fixed3),
            pl.BlockSpec((FFN_NCHUNK, 1, 2 * FFN_CHUNK), fixed3),
            pl.BlockSpec((FFN_DIM, D_MODEL), fixed2),
        ],
        out_specs=pl.BlockSpec((tm, D_MODEL), row),
        scratch_shapes=[
            pltpu.VMEM((tm, D_MODEL), BF16),
            pltpu.VMEM((2 * n_sl, tm + V7X_SUBLANES, V7X_LANES), F32),
            pltpu.VMEM((FFN_NCHUNK, n_sl, V7X_SUBLANES, V7X_LANES), F32),
            pltpu.VMEM((tm, FFN_DIM), BF16),
        ],
        compiler_params=_params(1),
        name="conv_ffn",
    )(x.reshape(m, D_MODEL), g_in, g_out, wup, wdw, bdw, wdn)
    return out.reshape(batch, seq, D_MODEL)


def _conformer_kernel(x_ref, g_in_ref, g_out_ref, w1_ref, b1_ref, wdw_ref, bdw_ref,
                      lng_ref, lnb_ref, w2_ref, b2_ref, out_ref, hn_ref, s_ref,
                      *, tiles_per_seq):
    tm = x_ref.shape[0]
    halo = CONV_HALO

    @pl.when(pl.program_id(0) % tiles_per_seq == 0)
    def _():
        s_ref[:, 0:halo, :] = jnp.zeros((D_SLABS, halo, V7X_LANES), F32)

    x = x_ref[...]
    hn_ref[...] = _rms(x, g_in_ref[...]).astype(BF16)

    first = halo - (CONV_KERNEL - 1)
    ys = []
    per_dot = 2
    for lb in range(D_SLABS):
        if lb % per_dot == 0:
            c0 = lb * V7X_LANES
            c1 = c0 + per_dot * V7X_LANES
            a = (jnp.dot(hn_ref[...], w1_ref[:, c0:c1], preferred_element_type=F32)
                 + b1_ref[:, c0:c1])
            gate = (jnp.dot(hn_ref[...], w1_ref[:, D_MODEL + c0:D_MODEL + c1],
                            preferred_element_type=F32)
                    + b1_ref[:, D_MODEL + c0:D_MODEL + c1])
            u = a * _sigmoid(gate)
        s_ref[lb, halo:halo + tm, :] = u[:, _slab(lb % per_dot)]
        chunks = []
        for rc in range(tm // CONV_ROWS):
            acc = jnp.zeros((CONV_ROWS, V7X_LANES), F32)
            for j in range(CONV_KERNEL):
                acc = acc + (s_ref[lb, pl.ds(rc * CONV_ROWS + first + j, CONV_ROWS), :]
                             * wdw_ref[lb, j:j + 1, :])
            chunks.append(acc)
        ys.append(jnp.concatenate(chunks, axis=0))
        s_ref[lb, 0:halo, :] = s_ref[lb, tm:tm + halo, :]

    v = jnp.concatenate(ys, axis=1) + bdw_ref[...]
    mu = jnp.mean(v, axis=-1, keepdims=True)
    vc = v - mu
    var = jnp.mean(vc * vc, axis=-1, keepdims=True)
    ln = vc * lax.rsqrt(var + EPS) * lng_ref[...] + lnb_ref[...]
    act = (ln * _sigmoid(ln)).astype(BF16)
    y = jnp.dot(act, w2_ref[...], preferred_element_type=F32) + b2_ref[...]
    out_ref[...] = x + _rms(y, g_out_ref[...])


def _conformer_call(x, g_in, g_out, w1, b1, wdw, bdw, lng, lnb, w2, b2):
    batch, seq, _ = x.shape
    m = batch * seq
    tm = ROW_TILE
    row = lambda i: (i, 0)
    fixed = lambda i: (0, 0)
    vec = pl.BlockSpec((1, D_MODEL), fixed)
    out = pl.pallas_call(
        functools.partial(_conformer_kernel, tiles_per_seq=seq // tm),
        out_shape=jax.ShapeDtypeStruct((m, D_MODEL), F32),
        grid=(m // tm,),
        in_specs=[
            pl.BlockSpec((tm, D_MODEL), row), vec, vec,
            pl.BlockSpec((D_MODEL, 2 * D_MODEL), fixed),
            pl.BlockSpec((1, 2 * D_MODEL), fixed),
            pl.BlockSpec((D_SLABS, CONV_HALO, V7X_LANES), lambda i: (0, 0, 0)),
            vec, vec, vec,
            pl.BlockSpec((D_MODEL, D_MODEL), fixed),
            vec,
        ],
        out_specs=pl.BlockSpec((tm, D_MODEL), row),
        scratch_shapes=[
            pltpu.VMEM((tm, D_MODEL), BF16),
            pltpu.VMEM((D_SLABS, tm + CONV_HALO, V7X_LANES), F32),
        ],
        compiler_params=_params(1),
        name="conformer_conv",
    )(x.reshape(m, D_MODEL), g_in, g_out, w1, b1, wdw, bdw, lng, lnb, w2, b2)
    return out.reshape(batch, seq, D_MODEL)


def _rope_tables():
    half = ROT_DIM // 2
    inv_freq = ROPE_THETA ** (-jnp.arange(half, dtype=F32) / half)
    d = jnp.arange(V7X_LANES) % HEAD_DIM
    invf = jnp.where(d < ROT_DIM, jnp.tile(inv_freq, V7X_LANES // half), 0.0)
    sgnm = jnp.where(d < half, -1.0, 0.0)
    sgnp = jnp.where((d >= half) & (d < ROT_DIM), 1.0, 0.0)
    as_row = lambda a: a.astype(F32).reshape(1, V7X_LANES)
    return as_row(invf), as_row(sgnm), as_row(sgnp)


def _ffn_weights(w_up, w_dw, b_dw, w_down):
    n, tf = FFN_NCHUNK, FFN_CHUNK
    wup = (w_up.reshape(D_MODEL, 2, n, tf).transpose(2, 0, 1, 3)
           .reshape(n, D_MODEL, 2 * tf).astype(BF16))
    wdw = w_dw.reshape(FFN_CONV, 2, n, tf).transpose(2, 0, 1, 3).reshape(n, FFN_CONV, 2 * tf)
    wdw = jnp.pad(wdw, ((0, 0), (0, V7X_SUBLANES - FFN_CONV), (0, 0)))
    bdw = b_dw.reshape(2, n, tf).transpose(1, 0, 2).reshape(n, 1, 2 * tf)
    return wup, wdw, bdw, w_down.astype(BF16)


def kernel(x, positions, norm_g, attn_w_qkv, attn_w_o, conv_w_pw1, conv_b_pw1, conv_w_dw, conv_b_dw, conv_ln_g, conv_ln_b, conv_w_pw2, conv_b_pw2, ffn_w_up, ffn_w_dw, ffn_b_dw, ffn_w_down):
    batch, seq, _ = x.shape
    pos = positions.reshape(batch, seq, 1)
    vec = lambda a: a.reshape(1, -1)
    for window, dilation in DILATED_GROUPS:
        assert window // dilation == SPAN

    g = norm_g[0]
    invf, sgnm, sgnp = _rope_tables()
    qkv_groups = _qkv_call(x, pos, vec(g[0]), invf, sgnm, sgnp, attn_w_qkv[0].astype(BF16))
    attn = [_attn_call(qkv_g) for qkv_g in qkv_groups]
    slot = jnp.arange(V7X_LANES)[:, None]
    expand = (jnp.arange(GROUP_WIDTH)[None, :] // HEAD_DIM == slot).astype(BF16)
    x = _merge_call([o for o, _ in attn], [l for _, l in attn], x,
                    attn_w_o[0].astype(BF16), vec(g[1]), expand)
    x = _ffn_call(x, vec(g[2]), vec(g[3]),
                  *_ffn_weights(ffn_w_up[0], ffn_w_dw[0], ffn_b_dw[0], ffn_w_down[0]))

    g = norm_g[1]
    wdw = jnp.pad(conv_w_dw[0], ((0, CONV_HALO - CONV_KERNEL), (0, 0)))
    wdw = wdw.reshape(CONV_HALO, D_SLABS, V7X_LANES).transpose(1, 0, 2)
    x = _conformer_call(x, vec(g[0]), vec(g[1]),
                        conv_w_pw1[0].astype(BF16), vec(conv_b_pw1[0]),
                        wdw, vec(conv_b_dw[0]), vec(conv_ln_g[0]), vec(conv_ln_b[0]),
                        conv_w_pw2[0].astype(BF16), vec(conv_b_pw2[0]))
    x = _ffn_call(x, vec(g[2]), vec(g[3]),
                  *_ffn_weights(ffn_w_up[1], ffn_w_dw[1], ffn_b_dw[1], ffn_w_down[1]))
    return x
```

```python
import functools

import jax
import jax.numpy as jnp
from jax import lax
from jax.experimental import pallas as pl
from jax.experimental.pallas import tpu as pltpu

D_MODEL = 1024
HEAD_DIM = 64
N_SLOTS = 8
DILATED_GROUPS = ((128, 1), (512, 4), (2048, 16))
N_GROUPS = 3
GROUP_WIDTH = N_SLOTS * HEAD_DIM
QKV_WIDTH = 3 * N_GROUPS * GROUP_WIDTH
ROT_DIM = HEAD_DIM // 4
ROPE_THETA = 500000.0
CONV_KERNEL = 31
FFN_DIM = 2816
FFN_CONV = 3
EPS = 1e-6

V7X_LANES = 128
V7X_SUBLANES = 8
V7X_VMEM_LIMIT_BYTES = 56 * 1024 * 1024

SPAN = 128
ROW_TILE = 512
ATT_TILE = 512
FFN_CHUNK = 256
FFN_NCHUNK = FFN_DIM // FFN_CHUNK
CONV_HALO = 32
CONV_ROWS = 64
NEG_BIG = -1e30
SLABS = GROUP_WIDTH // V7X_LANES
D_SLABS = D_MODEL // V7X_LANES

F32 = jnp.float32
BF16 = jnp.bfloat16


def _params(n_axes):
    return pltpu.CompilerParams(
        dimension_semantics=("arbitrary",) * n_axes,
        vmem_limit_bytes=V7X_VMEM_LIMIT_BYTES)


def _rms(x, g):
    return x * lax.rsqrt(jnp.mean(x * x, axis=-1, keepdims=True) + EPS) * g


def _sigmoid(x):
    return 1.0 / (1.0 + jnp.exp(-x))


def _slab(l):
    return slice(l * V7X_LANES, (l + 1) * V7X_LANES)


def _qkv_kernel(x_ref, pos_ref, g_ref, invf_ref, sgnm_ref, sgnp_ref, w_ref,
                o1_ref, o2_ref, o3_ref, hn_ref, perm_ref):
    tm = x_ref.shape[0]
    hn_ref[...] = _rms(x_ref[...], g_ref[...]).astype(BF16)
    ang = pos_ref[...].astype(F32) * invf_ref[...]
    cos = jnp.cos(ang)
    sin = jnp.sin(ang)
    sin_m = sin * sgnm_ref[...]
    sin_p = sin * sgnp_ref[...]
    qscale = HEAD_DIM ** -0.5
    outs = (o1_ref, o2_ref, o3_ref)
    for cb in range(QKV_WIDTH // GROUP_WIDTH):
        which, group = divmod(cb, N_GROUPS)
        dil = DILATED_GROUPS[group][1]
        o_ref = outs[group]
        res = jnp.dot(hn_ref[...], w_ref[:, cb * GROUP_WIDTH:(cb + 1) * GROUP_WIDTH],
                      preferred_element_type=F32)
        for l in range(SLABS):
            t = res[:, _slab(l)]
            if which < 2:
                t = (t * cos
                     + pltpu.roll(t, V7X_LANES - ROT_DIM // 2, 1) * sin_m
                     + pltpu.roll(t, ROT_DIM // 2, 1) * sin_p)
                if which == 0:
                    t = t * qscale
            cols = slice(which * GROUP_WIDTH + l * V7X_LANES,
                         which * GROUP_WIDTH + (l + 1) * V7X_LANES)
            if dil == 1:
                o_ref[0, :, cols] = t.astype(BF16)
            else:
                slot = (cb * SLABS + l) % perm_ref.shape[0]
                perm_ref[slot] = t
                for r in range(dil):
                    o_ref[r, :, cols] = perm_ref[slot, pl.ds(r, tm // dil, stride=dil), :].astype(BF16)


def _qkv_call(x, pos, g, invf, sgnm, sgnp, w):
    batch, seq, _ = x.shape
    tm = ROW_TILE
    row = lambda b, i: (b, i, 0)
    fixed = lambda b, i: (0, 0)
    out_shape, out_specs = [], []
    for _, dil in DILATED_GROUPS:
        out_shape.append(jax.ShapeDtypeStruct((batch, dil, seq // dil, 3 * GROUP_WIDTH), BF16))
        out_specs.append(pl.BlockSpec((None, dil, tm // dil, 3 * GROUP_WIDTH),
                                      lambda b, i: (b, 0, i, 0)))
    return pl.pallas_call(
        _qkv_kernel,
        out_shape=tuple(out_shape),
        grid=(batch, seq // tm),
        in_specs=[
            pl.BlockSpec((None, tm, D_MODEL), row),
            pl.BlockSpec((None, tm, 1), row),
            pl.BlockSpec((1, D_MODEL), fixed),
            pl.BlockSpec((1, V7X_LANES), fixed),
            pl.BlockSpec((1, V7X_LANES), fixed),
            pl.BlockSpec((1, V7X_LANES), fixed),
            pl.BlockSpec((D_MODEL, QKV_WIDTH), fixed),
        ],
        out_specs=tuple(out_specs),
        scratch_shapes=[pltpu.VMEM((tm, D_MODEL), BF16),
                        pltpu.VMEM((2 * SLABS, tm, V7X_LANES), F32)],
        compiler_params=_params(2),
        name="qkv_rope",
    )(x, pos, g, invf, sgnm, sgnp, w)


def _attn_kernel(q_ref, kc_ref, vc_ref, kp_ref, vp_ref, o_ref, lse_ref):
    jb = pl.program_id(2)
    lane = lax.broadcasted_iota(jnp.int32, (SPAN, V7X_LANES), 1)
    row = lax.broadcasted_iota(jnp.int32, (2 * SPAN, 2 * SPAN), 0) & (SPAN - 1)
    col = lax.broadcasted_iota(jnp.int32, (2 * SPAN, 2 * SPAN), 1)
    band = (col >= row) & (col <= row + SPAN)
    first = jnp.logical_and(band, jnp.logical_or(col >= SPAN, jb > 0))
    lo = lane < HEAD_DIM
    for i in range(ATT_TILE // SPAN):
        mask = first if i == 0 else band
        rows = slice(i * SPAN, (i + 1) * SPAN)
        lse_acc = jnp.zeros((SPAN, V7X_LANES), F32)
        for p in range(SLABS):
            cs = _slab(p)
            q2 = q_ref[rows, cs]
            if i == 0:
                kcat = jnp.concatenate([kp_ref[:, cs], kc_ref[0:SPAN, cs]], axis=0)
                vcat = jnp.concatenate([vp_ref[:, cs], vc_ref[0:SPAN, cs]], axis=0)
            else:
                kcat = kc_ref[(i - 1) * SPAN:(i + 1) * SPAN, cs]
                vcat = vc_ref[(i - 1) * SPAN:(i + 1) * SPAN, cs]
            zero = jnp.zeros_like(q2)
            qs = jnp.concatenate([jnp.where(lo, q2, zero), jnp.where(lo, zero, q2)],
                                 axis=0)
            s = lax.dot_general(qs, kcat, (((1,), (1,)), ((), ())),
                                preferred_element_type=F32)
            s = jnp.where(mask, s, NEG_BIG)
            mx = jnp.max(s, axis=-1, keepdims=True)
            pr = jnp.exp(s - mx)
            den = jnp.sum(pr, axis=-1, keepdims=True)
            pv = jnp.dot(pr.astype(BF16), vcat, preferred_element_type=F32)
            o_ref[rows, cs] = jnp.where(lo, pv[:SPAN], pv[SPAN:]).astype(BF16)
            lse_acc = jnp.where(lane == 2 * p, mx[:SPAN], lse_acc)
            lse_acc = jnp.where(lane == 2 * p + 1, mx[SPAN:], lse_acc)
            lse_acc = jnp.where(lane == N_SLOTS + 2 * p, den[:SPAN], lse_acc)
            lse_acc = jnp.where(lane == N_SLOTS + 2 * p + 1, den[SPAN:], lse_acc)
        lse_ref[rows, :] = lse_acc


def _attn_call(qkv_g):
    batch, dil, l, _ = qkv_g.shape
    sub = ATT_TILE // SPAN

    def spec(rows, which, prev):
        def imap(b, r, j):
            return (b, r, jnp.maximum(j * sub - 1, 0) if prev else j, which)
        return pl.BlockSpec((None, None, rows, GROUP_WIDTH), imap)

    here = lambda b, r, j: (b, r, j, 0)
    return pl.pallas_call(
        _attn_kernel,
        out_shape=(jax.ShapeDtypeStruct((batch, dil, l, GROUP_WIDTH), BF16),
                   jax.ShapeDtypeStruct((batch, dil, l, V7X_LANES), F32)),
        grid=(batch, dil, l // ATT_TILE),
        in_specs=[spec(ATT_TILE, 0, False), spec(ATT_TILE, 1, False),
                  spec(ATT_TILE, 2, False), spec(SPAN, 1, True), spec(SPAN, 2, True)],
        out_specs=(pl.BlockSpec((None, None, ATT_TILE, GROUP_WIDTH), here),
                   pl.BlockSpec((None, None, ATT_TILE, V7X_LANES), here)),
        compiler_params=_params(3),
        name=f"dilated_attn_d{dil}",
    )(qkv_g, qkv_g, qkv_g, qkv_g, qkv_g)


def _merge_kernel(o1_ref, o2_ref, o3_ref, l1_ref, l2_ref, l3_ref, x_ref, wo_ref,
                  g_ref, expand_ref, out_ref, os_ref, ls_ref):
    tm = x_ref.shape[0]
    o_tok, lses = [], []
    for gi, (o_ref, l_ref) in enumerate(((o1_ref, l1_ref), (o2_ref, l2_ref),
                                         (o3_ref, l3_ref))):
        dil = o_ref.shape[0]
        if dil == 1:
            o_tok.append(o_ref[0].astype(F32))
            lses.append(l_ref[0])
            continue
        for r in range(dil):
            rows = pl.ds(r, tm // dil, stride=dil)
            o_r = o_ref[r].astype(F32)
            for l in range(SLABS):
                os_ref[gi - 1, l, rows, :] = o_r[:, _slab(l)]
            ls_ref[gi - 1, rows, :] = l_ref[r]
        o_tok.append(jnp.concatenate([os_ref[gi - 1, l] for l in range(SLABS)], axis=1))
        lses.append(ls_ref[gi - 1])
    valid = lax.broadcasted_iota(jnp.int32, (tm, V7X_LANES), 1) < N_SLOTS
    mx = jnp.maximum(jnp.maximum(lses[0], lses[1]), lses[2])
    es = [jnp.exp(l - mx) for l in lses]
    dens = [pltpu.roll(l, V7X_LANES - N_SLOTS, 1) for l in lses]
    total = es[0] * dens[0] + es[1] * dens[1] + es[2] * dens[2]
    rden = 1.0 / jnp.where(valid, total, 1.0)
    mixed = None
    for e, o in zip(es, o_tok):
        w = jnp.where(valid, e * rden, 0.0)
        hi = w.astype(BF16)
        lo = (w - hi.astype(F32)).astype(BF16)
        wb = (jnp.dot(hi, expand_ref[...], preferred_element_type=F32)
              + jnp.dot(lo, expand_ref[...], preferred_element_type=F32))
        term = wb * o
        mixed = term if mixed is None else mixed + term
    y = jnp.dot(mixed.astype(BF16), wo_ref[...], preferred_element_type=F32)
    out_ref[...] = x_ref[...] + _rms(y, g_ref[...])


def _merge_call(os, lses, x, wo, g, expand):
    batch, seq, _ = x.shape
    tm = ROW_TILE
    row = lambda b, i: (b, i, 0)
    fixed = lambda b, i: (0, 0)
    phase = lambda b, i: (b, 0, i, 0)
    o_specs = [pl.BlockSpec((None, d, tm // d, GROUP_WIDTH), phase) for _, d in DILATED_GROUPS]
    l_specs = [pl.BlockSpec((None, d, tm // d, V7X_LANES), phase) for _, d in DILATED_GROUPS]
    return pl.pallas_call(
        _merge_kernel,
        out_shape=jax.ShapeDtypeStruct((batch, seq, D_MODEL), F32),
        grid=(batch, seq // tm),
        in_specs=o_specs + l_specs
        + [pl.BlockSpec((None, tm, D_MODEL), row),
           pl.BlockSpec((GROUP_WIDTH, D_MODEL), fixed),
           pl.BlockSpec((1, D_MODEL), fixed),
           pl.BlockSpec((V7X_LANES, GROUP_WIDTH), fixed)],
        out_specs=pl.BlockSpec((None, tm, D_MODEL), row),
        scratch_shapes=[pltpu.VMEM((N_GROUPS - 1, SLABS, tm, V7X_LANES), F32),
                        pltpu.VMEM((N_GROUPS - 1, tm, V7X_LANES), F32)],
        compiler_params=_params(2),
        name="attn_merge_out",
    )(*os, *lses, x, wo, g, expand)


def _ffn_kernel(x_ref, g_in_ref, g_out_ref, wup_ref, wdw_ref, bdw_ref, wdn_ref,
                out_ref, hn_ref, s_ref, carry_ref, act_ref, *, tiles_per_seq):
    tm = x_ref.shape[0]
    pad = V7X_SUBLANES
    n_sl = 2 * FFN_CHUNK // V7X_LANES

    @pl.when(pl.program_id(0) % tiles_per_seq == 0)
    def _():
        carry_ref[...] = jnp.zeros_like(carry_ref)

    hn_ref[...] = _rms(x_ref[...], g_in_ref[...]).astype(BF16)

    for c in range(FFN_NCHUNK):
        starts = (c * FFN_CHUNK, FFN_DIM + c * FFN_CHUNK)
        ug = [jnp.dot(hn_ref[...], wup_ref[:, s0:s0 + FFN_CHUNK], preferred_element_type=F32)
              for s0 in starts]
        conv = []
        for l in range(n_sl):
            half, hl = divmod(l, n_sl // 2)
            lanes = slice(starts[half] + hl * V7X_LANES, starts[half] + (hl + 1) * V7X_LANES)
            sl = s_ref.at[(c % 2) * n_sl + l]
            cur = ug[half][:, _slab(hl)]
            sl[0:pad, :] = carry_ref[c, l]
            sl[pad:pad + tm, :] = cur
            carry_ref[c, l] = cur[tm - pad:tm, :]
            conv.append(cur * wdw_ref[2:3, lanes]
                        + sl[pl.ds(pad - 1, tm), :] * wdw_ref[1:2, lanes]
                        + sl[pl.ds(pad - 2, tm), :] * wdw_ref[0:1, lanes]
                        + bdw_ref[:, lanes])
        for l in range(n_sl // 2):
            gate = conv[n_sl // 2 + l]
            act = gate * _sigmoid(gate) * conv[l]
            act_ref[:, c * FFN_CHUNK + l * V7X_LANES:c * FFN_CHUNK + (l + 1) * V7X_LANES] = (
                act.astype(BF16))

    y = jnp.dot(act_ref[...], wdn_ref[...], preferred_element_type=F32)
    out_ref[...] = x_ref[...] + _rms(y, g_out_ref[...])


def _ffn_call(x, g_in, g_out, wup, wdw, bdw, wdn):
    batch, seq, _ = x.shape
    m = batch * seq
    tm = ROW_TILE
    n_sl = 2 * FFN_CHUNK // V7X_LANES
    row = lambda i: (i, 0)
    fixed2 = lambda i: (0, 0)
    out = pl.pallas_call(
        functools.partial(_ffn_kernel, tiles_per_seq=seq // tm),
        out_shape=jax.ShapeDtypeStruct((m, D_MODEL), F32),
        grid=(m // tm,),
        in_specs=[
            pl.BlockSpec((tm, D_MODEL), row),
            pl.BlockSpec((1, D_MODEL), fixed2),
            pl.BlockSpec((1, D_MODEL), fixed2),
            pl.BlockSpec((D_MODEL, 2 * FFN_DIM), fixed2),
            pl.BlockSpec((FFN_CONV, 2 * FFN_DIM), fixed2),
            pl.BlockSpec((1, 2 * FFN_DIM), fixed2),
            pl.BlockSpec((FFN_DIM, D_MODEL), fixed2),
        ],
        out_specs=pl.BlockSpec((tm, D_MODEL), row),
        scratch_shapes=[
            pltpu.VMEM((tm, D_MODEL), BF16),
            pltpu.VMEM((2 * n_sl, tm + V7X_SUBLANES, V7X_LANES), F32),
            pltpu.VMEM((FFN_NCHUNK, n_sl, V7X_SUBLANES, V7X_LANES), F32),
            pltpu.VMEM((tm, FFN_DIM), BF16),
        ],
        compiler_params=_params(1),
        name="conv_ffn",
    )(x.reshape(m, D_MODEL), g_in, g_out, wup, wdw, bdw, wdn)
    return out.reshape(batch, seq, D_MODEL)


def _conformer_kernel(x_ref, g_in_ref, g_out_ref, w1_ref, b1_ref, wdw_ref, bdw_ref,
                      lng_ref, lnb_ref, w2_ref, b2_ref, out_ref, hn_ref, s_ref,
                      *, tiles_per_seq):
    tm = x_ref.shape[0]
    halo = CONV_HALO

    @pl.when(pl.program_id(0) % tiles_per_seq == 0)
    def _():
        s_ref[:, 0:halo, :] = jnp.zeros((D_SLABS, halo, V7X_LANES), F32)

    x = x_ref[...]
    hn_ref[...] = _rms(x, g_in_ref[...]).astype(BF16)

    first = halo - (CONV_KERNEL - 1)
    ys = []
    per_dot = 2
    for lb in range(D_SLABS):
        if lb % per_dot == 0:
            c0 = lb * V7X_LANES
            c1 = c0 + per_dot * V7X_LANES
            a = (jnp.dot(hn_ref[...], w1_ref[:, c0:c1], preferred_element_type=F32)
                 + b1_ref[:, c0:c1])
            gate = (jnp.dot(hn_ref[...], w1_ref[:, D_MODEL + c0:D_MODEL + c1],
                            preferred_element_type=F32)
                    + b1_ref[:, D_MODEL + c0:D_MODEL + c1])
            u = a * _sigmoid(gate)
        s_ref[lb, halo:halo + tm, :] = u[:, _slab(lb % per_dot)]
        chunks = []
        for rc in range(tm // CONV_ROWS):
            acc = jnp.zeros((CONV_ROWS, V7X_LANES), F32)
            for j in range(CONV_KERNEL):
                acc = acc + (s_ref[lb, pl.ds(rc * CONV_ROWS + first + j, CONV_ROWS), :]
                             * wdw_ref[lb, j:j + 1, :])
            chunks.append(acc)
        ys.append(jnp.concatenate(chunks, axis=0))
        s_ref[lb, 0:halo, :] = s_ref[lb, tm:tm + halo, :]

    v = jnp.concatenate(ys, axis=1) + bdw_ref[...]
    mu = jnp.mean(v, axis=-1, keepdims=True)
    vc = v - mu
    var = jnp.mean(vc * vc, axis=-1, keepdims=True)
    ln = vc * lax.rsqrt(var + EPS) * lng_ref[...] + lnb_ref[...]
    act = (ln * _sigmoid(ln)).astype(BF16)
    y = jnp.dot(act, w2_ref[...], preferred_element_type=F32) + b2_ref[...]
    out_ref[...] = x + _rms(y, g_out_ref[...])


def _conformer_call(x, g_in, g_out, w1, b1, wdw, bdw, lng, lnb, w2, b2):
    batch, seq, _ = x.shape
    m = batch * seq
    tm = ROW_TILE
    row = lambda i: (i, 0)
    fixed = lambda i: (0, 0)
    vec = pl.BlockSpec((1, D_MODEL), fixed)
    out = pl.pallas_call(
        functools.partial(_conformer_kernel, tiles_per_seq=seq // tm),
        out_shape=jax.ShapeDtypeStruct((m, D_MODEL), F32),
        grid=(m // tm,),
        in_specs=[
            pl.BlockSpec((tm, D_MODEL), row), vec, vec,
            pl.BlockSpec((D_MODEL, 2 * D_MODEL), fixed),
            pl.BlockSpec((1, 2 * D_MODEL), fixed),
            pl.BlockSpec((D_SLABS, CONV_HALO, V7X_LANES), lambda i: (0, 0, 0)),
            vec, vec, vec,
            pl.BlockSpec((D_MODEL, D_MODEL), fixed),
            vec,
        ],
        out_specs=pl.BlockSpec((tm, D_MODEL), row),
        scratch_shapes=[
            pltpu.VMEM((tm, D_MODEL), BF16),
            pltpu.VMEM((D_SLABS, tm + CONV_HALO, V7X_LANES), F32),
        ],
        compiler_params=_params(1),
        name="conformer_conv",
    )(x.reshape(m, D_MODEL), g_in, g_out, w1, b1, wdw, bdw, lng, lnb, w2, b2)
    return out.reshape(batch, seq, D_MODEL)


def _rope_tables():
    half = ROT_DIM // 2
    inv_freq = ROPE_THETA ** (-jnp.arange(half, dtype=F32) / half)
    d = jnp.arange(V7X_LANES) % HEAD_DIM
    invf = jnp.where(d < ROT_DIM, jnp.tile(inv_freq, V7X_LANES // half), 0.0)
    sgnm = jnp.where(d < half, -1.0, 0.0)
    sgnp = jnp.where((d >= half) & (d < ROT_DIM), 1.0, 0.0)
    as_row = lambda a: a.astype(F32).reshape(1, V7X_LANES)
    return as_row(invf), as_row(sgnm), as_row(sgnp)


def _ffn_weights(w_up, w_dw, b_dw, w_down):
    return w_up.astype(BF16), w_dw, b_dw.reshape(1, -1), w_down.astype(BF16)


def kernel(x, positions, norm_g, attn_w_qkv, attn_w_o, conv_w_pw1, conv_b_pw1, conv_w_dw, conv_b_dw, conv_ln_g, conv_ln_b, conv_w_pw2, conv_b_pw2, ffn_w_up, ffn_w_dw, ffn_b_dw, ffn_w_down):
    batch, seq, _ = x.shape
    pos = positions.reshape(batch, seq, 1)
    vec = lambda a: a.reshape(1, -1)
    for window, dilation in DILATED_GROUPS:
        assert window // dilation == SPAN

    g = norm_g[0]
    invf, sgnm, sgnp = _rope_tables()
    qkv_groups = _qkv_call(x, pos, vec(g[0]), invf, sgnm, sgnp, attn_w_qkv[0].astype(BF16))
    attn = [_attn_call(qkv_g) for qkv_g in qkv_groups]
    slot = jnp.arange(V7X_LANES)[:, None]
    expand = (jnp.arange(GROUP_WIDTH)[None, :] // HEAD_DIM == slot).astype(BF16)
    x = _merge_call([o for o, _ in attn], [l for _, l in attn], x,
                    attn_w_o[0].astype(BF16), vec(g[1]), expand)
    x = _ffn_call(x, vec(g[2]), vec(g[3]),
                  *_ffn_weights(ffn_w_up[0], ffn_w_dw[0], ffn_b_dw[0], ffn_w_down[0]))

    g = norm_g[1]
    wdw = jnp.pad(conv_w_dw[0], ((0, CONV_HALO - CONV_KERNEL), (0, 0)))
    wdw = wdw.reshape(CONV_HALO, D_SLABS, V7X_LANES).transpose(1, 0, 2)
    x = _conformer_call(x, vec(g[0]), vec(g[1]),
                        conv_w_pw1[0].astype(BF16), vec(conv_b_pw1[0]),
                        wdw, vec(conv_b_dw[0]), vec(conv_ln_g[0]), vec(conv_ln_b[0]),
                        conv_w_pw2[0].astype(BF16), vec(conv_b_pw2[0]))
    x = _ffn_call(x, vec(g[2]), vec(g[3]),
                  *_ffn_weights(ffn_w_up[1], ffn_w_dw[1], ffn_b_dw[1], ffn_w_down[1]))
    return x
```

```python
import functools

import jax
import jax.numpy as jnp
from jax import lax
from jax.experimental import pallas as pl
from jax.experimental.pallas import tpu as pltpu

D_MODEL = 1024
HEAD_DIM = 64
N_SLOTS = 8
DILATED_GROUPS = ((128, 1), (512, 4), (2048, 16))
N_GROUPS = 3
GROUP_WIDTH = N_SLOTS * HEAD_DIM
QKV_WIDTH = 3 * N_GROUPS * GROUP_WIDTH
ROT_DIM = HEAD_DIM // 4
ROPE_THETA = 500000.0
CONV_KERNEL = 31
FFN_DIM = 2816
FFN_CONV = 3
EPS = 1e-6

V7X_LANES = 128
V7X_SUBLANES = 8
V7X_VMEM_LIMIT_BYTES = 56 * 1024 * 1024

SPAN = 128
ROW_TILE = 512
ATT_TILE = 512
FFN_CHUNK = 256
FFN_NCHUNK = FFN_DIM // FFN_CHUNK
CONV_HALO = 32
CONV_ROWS = 64
NEG_BIG = -1e30
LOG2_E = 1.4426950408889634
SLABS = GROUP_WIDTH // V7X_LANES
D_SLABS = D_MODEL // V7X_LANES

F32 = jnp.float32
BF16 = jnp.bfloat16


def _params(n_axes):
    return pltpu.CompilerParams(
        dimension_semantics=("arbitrary",) * n_axes,
        vmem_limit_bytes=V7X_VMEM_LIMIT_BYTES)


def _rms(x, g):
    return x * lax.rsqrt(jnp.mean(x * x, axis=-1, keepdims=True) + EPS) * g


def _sigmoid(x):
    return 1.0 / (1.0 + jnp.exp(-x))


def _slab(l):
    return slice(l * V7X_LANES, (l + 1) * V7X_LANES)


def _qkv_kernel(x_ref, pos_ref, g_ref, invf_ref, sgnm_ref, sgnp_ref, w_ref,
                o1_ref, o2_ref, o3_ref, hn_ref, perm_ref):
    tm = x_ref.shape[0]
    hn_ref[...] = _rms(x_ref[...], g_ref[...]).astype(BF16)
    ang = pos_ref[...].astype(F32) * invf_ref[...]
    cos = jnp.cos(ang)
    sin = jnp.sin(ang)
    sin_m = sin * sgnm_ref[...]
    sin_p = sin * sgnp_ref[...]
    qscale = HEAD_DIM ** -0.5 * LOG2_E
    outs = (o1_ref, o2_ref, o3_ref)
    for cb in range(QKV_WIDTH // GROUP_WIDTH):
        which, group = divmod(cb, N_GROUPS)
        dil = DILATED_GROUPS[group][1]
        o_ref = outs[group]
        res = jnp.dot(hn_ref[...], w_ref[:, cb * GROUP_WIDTH:(cb + 1) * GROUP_WIDTH],
                      preferred_element_type=F32)
        for l in range(SLABS):
            t = res[:, _slab(l)]
            if which < 2:
                t = (t * cos
                     + pltpu.roll(t, V7X_LANES - ROT_DIM // 2, 1) * sin_m
                     + pltpu.roll(t, ROT_DIM // 2, 1) * sin_p)
                if which == 0:
                    t = t * qscale
            cols = slice(which * GROUP_WIDTH + l * V7X_LANES,
                         which * GROUP_WIDTH + (l + 1) * V7X_LANES)
            if dil == 1:
                o_ref[0, :, cols] = t.astype(BF16)
            else:
                slot = (cb * SLABS + l) % perm_ref.shape[0]
                perm_ref[slot] = t
                for r in range(dil):
                    o_ref[r, :, cols] = perm_ref[slot, pl.ds(r, tm // dil, stride=dil), :].astype(BF16)


def _qkv_call(x, pos, g, invf, sgnm, sgnp, w):
    batch, seq, _ = x.shape
    tm = ROW_TILE
    row = lambda b, i: (b, i, 0)
    fixed = lambda b, i: (0, 0)
    out_shape, out_specs = [], []
    for _, dil in DILATED_GROUPS:
        out_shape.append(jax.ShapeDtypeStruct((batch, dil, seq // dil, 3 * GROUP_WIDTH), BF16))
        out_specs.append(pl.BlockSpec((None, dil, tm // dil, 3 * GROUP_WIDTH),
                                      lambda b, i: (b, 0, i, 0)))
    return pl.pallas_call(
        _qkv_kernel,
        out_shape=tuple(out_shape),
        grid=(batch, seq // tm),
        in_specs=[
            pl.BlockSpec((None, tm, D_MODEL), row),
            pl.BlockSpec((None, tm, 1), row),
            pl.BlockSpec((1, D_MODEL), fixed),
            pl.BlockSpec((1, V7X_LANES), fixed),
            pl.BlockSpec((1, V7X_LANES), fixed),
            pl.BlockSpec((1, V7X_LANES), fixed),
            pl.BlockSpec((D_MODEL, QKV_WIDTH), fixed),
        ],
        out_specs=tuple(out_specs),
        scratch_shapes=[pltpu.VMEM((tm, D_MODEL), BF16),
                        pltpu.VMEM((2 * SLABS, tm, V7X_LANES), F32)],
        compiler_params=_params(2),
        name="qkv_rope",
    )(x, pos, g, invf, sgnm, sgnp, w)


def _attn_kernel(q_ref, kc_ref, vc_ref, kp_ref, vp_ref, o_ref, lse_ref):
    jb = pl.program_id(2)
    lane = lax.broadcasted_iota(jnp.int32, (SPAN, V7X_LANES), 1)
    row = lax.broadcasted_iota(jnp.int32, (2 * SPAN, 2 * SPAN), 0) & (SPAN - 1)
    col = lax.broadcasted_iota(jnp.int32, (2 * SPAN, 2 * SPAN), 1)
    band = (col >= row) & (col <= row + SPAN)
    first = jnp.logical_and(band, jnp.logical_or(col >= SPAN, jb > 0))
    band_bias = jnp.where(band, 0.0, NEG_BIG)
    first_bias = jnp.where(first, 0.0, NEG_BIG)
    lo = lane < HEAD_DIM
    for i in range(ATT_TILE // SPAN):
        bias = first_bias if i == 0 else band_bias
        rows = slice(i * SPAN, (i + 1) * SPAN)
        lse_acc = jnp.zeros((SPAN, V7X_LANES), F32)
        for p in range(SLABS):
            cs = _slab(p)
            q2 = q_ref[rows, cs]
            if i == 0:
                kcat = jnp.concatenate([kp_ref[:, cs], kc_ref[0:SPAN, cs]], axis=0)
                vcat = jnp.concatenate([vp_ref[:, cs], vc_ref[0:SPAN, cs]], axis=0)
            else:
                kcat = kc_ref[(i - 1) * SPAN:(i + 1) * SPAN, cs]
                vcat = vc_ref[(i - 1) * SPAN:(i + 1) * SPAN, cs]
            zero = jnp.zeros_like(q2)
            qs = jnp.concatenate([jnp.where(lo, q2, zero), jnp.where(lo, zero, q2)],
                                 axis=0)
            s = lax.dot_general(qs, kcat, (((1,), (1,)), ((), ())),
                                preferred_element_type=F32)
            s = s + bias
            mx = jnp.max(s, axis=-1, keepdims=True)
            pr = jnp.exp2(s - mx)
            den = jnp.sum(pr, axis=-1, keepdims=True)
            pv = jnp.dot(pr.astype(BF16), vcat, preferred_element_type=F32)
            o_ref[rows, cs] = jnp.where(lo, pv[:SPAN], pv[SPAN:]).astype(BF16)
            lse_acc = jnp.where(lane == 2 * p, mx[:SPAN], lse_acc)
            lse_acc = jnp.where(lane == 2 * p + 1, mx[SPAN:], lse_acc)
            lse_acc = jnp.where(lane == N_SLOTS + 2 * p, den[:SPAN], lse_acc)
            lse_acc = jnp.where(lane == N_SLOTS + 2 * p + 1, den[SPAN:], lse_acc)
        lse_ref[rows, :] = lse_acc


def _attn_call(qkv_g):
    batch, dil, l, _ = qkv_g.shape
    sub = ATT_TILE // SPAN

    def spec(rows, which, prev):
        def imap(b, r, j):
            return (b, r, jnp.maximum(j * sub - 1, 0) if prev else j, which)
        return pl.BlockSpec((None, None, rows, GROUP_WIDTH), imap)

    here = lambda b, r, j: (b, r, j, 0)
    return pl.pallas_call(
        _attn_kernel,
        out_shape=(jax.ShapeDtypeStruct((batch, dil, l, GROUP_WIDTH), BF16),
                   jax.ShapeDtypeStruct((batch, dil, l, V7X_LANES), F32)),
        grid=(batch, dil, l // ATT_TILE),
        in_specs=[spec(ATT_TILE, 0, False), spec(ATT_TILE, 1, False),
                  spec(ATT_TILE, 2, False), spec(SPAN, 1, True), spec(SPAN, 2, True)],
        out_specs=(pl.BlockSpec((None, None, ATT_TILE, GROUP_WIDTH), here),
                   pl.BlockSpec((None, None, ATT_TILE, V7X_LANES), here)),
        compiler_params=_params(3),
        name=f"dilated_attn_d{dil}",
    )(qkv_g, qkv_g, qkv_g, qkv_g, qkv_g)


def _merge_kernel(o1_ref, o2_ref, o3_ref, l1_ref, l2_ref, l3_ref, x_ref, wo_ref,
                  g_ref, expand_ref, out_ref, os_ref, ls_ref):
    tm = x_ref.shape[0]
    o_tok, lses = [], []
    for gi, (o_ref, l_ref) in enumerate(((o1_ref, l1_ref), (o2_ref, l2_ref),
                                         (o3_ref, l3_ref))):
        dil = o_ref.shape[0]
        if dil == 1:
            o_tok.append(o_ref[0].astype(F32))
            lses.append(l_ref[0])
            continue
        for r in range(dil):
            rows = pl.ds(r, tm // dil, stride=dil)
            o_r = o_ref[r].astype(F32)
            for l in range(SLABS):
                os_ref[gi - 1, l, rows, :] = o_r[:, _slab(l)]
            ls_ref[gi - 1, rows, :] = l_ref[r]
        o_tok.append(jnp.concatenate([os_ref[gi - 1, l] for l in range(SLABS)], axis=1))
        lses.append(ls_ref[gi - 1])
    valid = lax.broadcasted_iota(jnp.int32, (tm, V7X_LANES), 1) < N_SLOTS
    mx = jnp.maximum(jnp.maximum(lses[0], lses[1]), lses[2])
    es = [jnp.exp2(l - mx) for l in lses]
    dens = [pltpu.roll(l, V7X_LANES - N_SLOTS, 1) for l in lses]
    total = es[0] * dens[0] + es[1] * dens[1] + es[2] * dens[2]
    rden = 1.0 / jnp.where(valid, total, 1.0)
    mixed = None
    for e, o in zip(es, o_tok):
        w = jnp.where(valid, e * rden, 0.0)
        hi = w.astype(BF16)
        lo = (w - hi.astype(F32)).astype(BF16)
        wb = (jnp.dot(hi, expand_ref[...], preferred_element_type=F32)
              + jnp.dot(lo, expand_ref[...], preferred_element_type=F32))
        term = wb * o
        mixed = term if mixed is None else mixed + term
    y = jnp.dot(mixed.astype(BF16), wo_ref[...], preferred_element_type=F32)
    out_ref[...] = x_ref[...] + _rms(y, g_ref[...])


def _merge_call(os, lses, x, wo, g, expand):
    batch, seq, _ = x.shape
    tm = ROW_TILE
    row = lambda b, i: (b, i, 0)
    fixed = lambda b, i: (0, 0)
    phase = lambda b, i: (b, 0, i, 0)
    o_specs = [pl.BlockSpec((None, d, tm // d, GROUP_WIDTH), phase) for _, d in DILATED_GROUPS]
    l_specs = [pl.BlockSpec((None, d, tm // d, V7X_LANES), phase) for _, d in DILATED_GROUPS]
    return pl.pallas_call(
        _merge_kernel,
        out_shape=jax.ShapeDtypeStruct((batch, seq, D_MODEL), F32),
        grid=(batch, seq // tm),
        in_specs=o_specs + l_specs
        + [pl.BlockSpec((None, tm, D_MODEL), row),
           pl.BlockSpec((GROUP_WIDTH, D_MODEL), fixed),
           pl.BlockSpec((1, D_MODEL), fixed),
           pl.BlockSpec((V7X_LANES, GROUP_WIDTH), fixed)],
        out_specs=pl.BlockSpec((None, tm, D_MODEL), row),
        scratch_shapes=[pltpu.VMEM((N_GROUPS - 1, SLABS, tm, V7X_LANES), F32),
                        pltpu.VMEM((N_GROUPS - 1, tm, V7X_LANES), F32)],
        compiler_params=_params(2),
        name="attn_merge_out",
    )(*os, *lses, x, wo, g, expand)


def _ffn_kernel(x_ref, g_in_ref, g_out_ref, wup_ref, wdw_ref, bdw_ref, wdn_ref,
                out_ref, hn_ref, s_ref, carry_ref, act_ref, *, tiles_per_seq):
    tm = x_ref.shape[0]
    pad = V7X_SUBLANES
    n_sl = 2 * FFN_CHUNK // V7X_LANES

    @pl.when(pl.program_id(0) % tiles_per_seq == 0)
    def _():
        carry_ref[...] = jnp.zeros_like(carry_ref)

    hn_ref[...] = _rms(x_ref[...], g_in_ref[...]).astype(BF16)

    for c in range(FFN_NCHUNK):
        starts = (c * FFN_CHUNK, FFN_DIM + c * FFN_CHUNK)
        ug = [jnp.dot(hn_ref[...], wup_ref[:, s0:s0 + FFN_CHUNK], preferred_element_type=F32)
              for s0 in starts]
        conv = []
        for l in range(n_sl):
            half, hl = divmod(l, n_sl // 2)
            lanes = slice(starts[half] + hl * V7X_LANES, starts[half] + (hl + 1) * V7X_LANES)
            sl = s_ref.at[(c % 2) * n_sl + l]
            cur = ug[half][:, _slab(hl)]
            sl[0:pad, :] = carry_ref[c, l]
            sl[pad:pad + tm, :] = cur
            carry_ref[c, l] = cur[tm - pad:tm, :]
            conv.append(cur * wdw_ref[2:3, lanes]
                        + sl[pl.ds(pad - 1, tm), :] * wdw_ref[1:2, lanes]
                        + sl[pl.ds(pad - 2, tm), :] * wdw_ref[0:1, lanes]
                        + bdw_ref[:, lanes])
        for l in range(n_sl // 2):
            gate = conv[n_sl // 2 + l]
            act = gate * _sigmoid(gate) * conv[l]
            act_ref[:, c * FFN_CHUNK + l * V7X_LANES:c * FFN_CHUNK + (l + 1) * V7X_LANES] = (
                act.astype(BF16))

    y = jnp.dot(act_ref[...], wdn_ref[...], preferred_element_type=F32)
    out_ref[...] = x_ref[...] + _rms(y, g_out_ref[...])


def _ffn_call(x, g_in, g_out, wup, wdw, bdw, wdn):
    batch, seq, _ = x.shape
    m = batch * seq
    tm = ROW_TILE
    n_sl = 2 * FFN_CHUNK // V7X_LANES
    row = lambda i: (i, 0)
    fixed2 = lambda i: (0, 0)
    out = pl.pallas_call(
        functools.partial(_ffn_kernel, tiles_per_seq=seq // tm),
        out_shape=jax.ShapeDtypeStruct((m, D_MODEL), F32),
        grid=(m // tm,),
        in_specs=[
            pl.BlockSpec((tm, D_MODEL), row),
            pl.BlockSpec((1, D_MODEL), fixed2),
            pl.BlockSpec((1, D_MODEL), fixed2),
            pl.BlockSpec((D_MODEL, 2 * FFN_DIM), fixed2),
            pl.BlockSpec((FFN_CONV, 2 * FFN_DIM), fixed2),
            pl.BlockSpec((1, 2 * FFN_DIM), fixed2),
            pl.BlockSpec((FFN_DIM, D_MODEL), fixed2),
        ],
        out_specs=pl.BlockSpec((tm, D_MODEL), row),
        scratch_shapes=[
            pltpu.VMEM((tm, D_MODEL), BF16),
            pltpu.VMEM((2 * n_sl, tm + V7X_SUBLANES, V7X_LANES), F32),
            pltpu.VMEM((FFN_NCHUNK, n_sl, V7X_SUBLANES, V7X_LANES), F32),
            pltpu.VMEM((tm, FFN_DIM), BF16),
        ],
        compiler_params=_params(1),
        name="conv_ffn",
    )(x.reshape(m, D_MODEL), g_in, g_out, wup, wdw, bdw, wdn)
    return out.reshape(batch, seq, D_MODEL)


def _conformer_kernel(x_ref, g_in_ref, g_out_ref, w1_ref, b1_ref, wdw_ref, bdw_ref,
                      lng_ref, lnb_ref, w2_ref, b2_ref, out_ref, hn_ref, s_ref,
                      *, tiles_per_seq):
    tm = x_ref.shape[0]
    halo = CONV_HALO

    @pl.when(pl.program_id(0) % tiles_per_seq == 0)
    def _():
        s_ref[:, 0:halo, :] = jnp.zeros((D_SLABS, halo, V7X_LANES), F32)

    x = x_ref[...]
    hn_ref[...] = _rms(x, g_in_ref[...]).astype(BF16)

    first = halo - (CONV_KERNEL - 1)
    ys = []
    per_dot = 2
    for lb in range(D_SLABS):
        if lb % per_dot == 0:
            c0 = lb * V7X_LANES
            c1 = c0 + per_dot * V7X_LANES
            a = (jnp.dot(hn_ref[...], w1_ref[:, c0:c1], preferred_element_type=F32)
                 + b1_ref[:, c0:c1])
            gate = (jnp.dot(hn_ref[...], w1_ref[:, D_MODEL + c0:D_MODEL + c1],
                            preferred_element_type=F32)
                    + b1_ref[:, D_MODEL + c0:D_MODEL + c1])
            u = a * _sigmoid(gate)
        s_ref[lb, halo:halo + tm, :] = u[:, _slab(lb % per_dot)]
        chunks = []
        for rc in range(tm // CONV_ROWS):
            acc = jnp.zeros((CONV_ROWS, V7X_LANES), F32)
            for j in range(CONV_KERNEL):
                acc = acc + (s_ref[lb, pl.ds(rc * CONV_ROWS + first + j, CONV_ROWS), :]
                             * wdw_ref[lb, j:j + 1, :])
            chunks.append(acc)
        ys.append(jnp.concatenate(chunks, axis=0))
        s_ref[lb, 0:halo, :] = s_ref[lb, tm:tm + halo, :]

    v = jnp.concatenate(ys, axis=1) + bdw_ref[...]
    mu = jnp.mean(v, axis=-1, keepdims=True)
    vc = v - mu
    var = jnp.mean(vc * vc, axis=-1, keepdims=True)
    ln = vc * lax.rsqrt(var + EPS) * lng_ref[...] + lnb_ref[...]
    act = (ln * _sigmoid(ln)).astype(BF16)
    y = jnp.dot(act, w2_ref[...], preferred_element_type=F32) + b2_ref[...]
    out_ref[...] = x + _rms(y, g_out_ref[...])


def _conformer_call(x, g_in, g_out, w1, b1, wdw, bdw, lng, lnb, w2, b2):
    batch, seq, _ = x.shape
    m = batch * seq
    tm = ROW_TILE
    row = lambda i: (i, 0)
    fixed = lambda i: (0, 0)
    vec = pl.BlockSpec((1, D_MODEL), fixed)
    out = pl.pallas_call(
        functools.partial(_conformer_kernel, tiles_per_seq=seq // tm),
        out_shape=jax.ShapeDtypeStruct((m, D_MODEL), F32),
        grid=(m // tm,),
        in_specs=[
            pl.BlockSpec((tm, D_MODEL), row), vec, vec,
            pl.BlockSpec((D_MODEL, 2 * D_MODEL), fixed),
            pl.BlockSpec((1, 2 * D_MODEL), fixed),
            pl.BlockSpec((D_SLABS, CONV_HALO, V7X_LANES), lambda i: (0, 0, 0)),
            vec, vec, vec,
            pl.BlockSpec((D_MODEL, D_MODEL), fixed),
            vec,
        ],
        out_specs=pl.BlockSpec((tm, D_MODEL), row),
        scratch_shapes=[
            pltpu.VMEM((tm, D_MODEL), BF16),
            pltpu.VMEM((D_SLABS, tm + CONV_HALO, V7X_LANES), F32),
        ],
        compiler_params=_params(1),
        name="conformer_conv",
    )(x.reshape(m, D_MODEL), g_in, g_out, w1, b1, wdw, bdw, lng, lnb, w2, b2)
    return out.reshape(batch, seq, D_MODEL)


def _rope_tables():
    half = ROT_DIM // 2
    inv_freq = ROPE_THETA ** (-jnp.arange(half, dtype=F32) / half)
    d = jnp.arange(V7X_LANES) % HEAD_DIM
    invf = jnp.where(d < ROT_DIM, jnp.tile(inv_freq, V7X_LANES // half), 0.0)
    sgnm = jnp.where(d < half, -1.0, 0.0)
    sgnp = jnp.where((d >= half) & (d < ROT_DIM), 1.0, 0.0)
    as_row = lambda a: a.astype(F32).reshape(1, V7X_LANES)
    return as_row(invf), as_row(sgnm), as_row(sgnp)


def _ffn_weights(w_up, w_dw, b_dw, w_down):
    return w_up.astype(BF16), w_dw, b_dw.reshape(1, -1), w_down.astype(BF16)


def kernel(x, positions, norm_g, attn_w_qkv, attn_w_o, conv_w_pw1, conv_b_pw1, conv_w_dw, conv_b_dw, conv_ln_g, conv_ln_b, conv_w_pw2, conv_b_pw2, ffn_w_up, ffn_w_dw, ffn_b_dw, ffn_w_down):
    batch, seq, _ = x.shape
    pos = positions.reshape(batch, seq, 1)
    vec = lambda a: a.reshape(1, -1)
    for window, dilation in DILATED_GROUPS:
        assert window // dilation == SPAN

    g = norm_g[0]
    invf, sgnm, sgnp = _rope_tables()
    qkv_groups = _qkv_call(x, pos, vec(g[0]), invf, sgnm, sgnp, attn_w_qkv[0].astype(BF16))
    attn = [_attn_call(qkv_g) for qkv_g in qkv_groups]
    slot = jnp.arange(V7X_LANES)[:, None]
    expand = (jnp.arange(GROUP_WIDTH)[None, :] // HEAD_DIM == slot).astype(BF16)
    x = _merge_call([o for o, _ in attn], [l for _, l in attn], x,
                    attn_w_o[0].astype(BF16), vec(g[1]), expand)
    x = _ffn_call(x, vec(g[2]), vec(g[3]),
                  *_ffn_weights(ffn_w_up[0], ffn_w_dw[0], ffn_b_dw[0], ffn_w_down[0]))

    g = norm_g[1]
    wdw = jnp.pad(conv_w_dw[0], ((0, CONV_HALO - CONV_KERNEL), (0, 0)))
    wdw = wdw.reshape(CONV_HALO, D_SLABS, V7X_LANES).transpose(1, 0, 2)
    x = _conformer_call(x, vec(g[0]), vec(g[1]),
                        conv_w_pw1[0].astype(BF16), vec(conv_b_pw1[0]),
                        wdw, vec(conv_b_dw[0]), vec(conv_ln_g[0]), vec(conv_ln_b[0]),
                        conv_w_pw2[0].astype(BF16), vec(conv_b_pw2[0]))
    x = _ffn_call(x, vec(g[2]), vec(g[3]),
                  *_ffn_weights(ffn_w_up[1], ffn_w_dw[1], ffn_b_dw[1], ffn_w_down[1]))
    return x
```

```python
import functools

import jax
import jax.numpy as jnp
from jax import lax
from jax.experimental import pallas as pl
from jax.experimental.pallas import tpu as pltpu

D_MODEL = 1024
HEAD_DIM = 64
N_SLOTS = 8
DILATED_GROUPS = ((128, 1), (512, 4), (2048, 16))
N_GROUPS = 3
GROUP_WIDTH = N_SLOTS * HEAD_DIM
QKV_WIDTH = 3 * N_GROUPS * GROUP_WIDTH
ROT_DIM = HEAD_DIM // 4
ROPE_THETA = 500000.0
CONV_KERNEL = 31
FFN_DIM = 2816
FFN_CONV = 3
EPS = 1e-6

V7X_LANES = 128
V7X_SUBLANES = 8
V7X_VMEM_LIMIT_BYTES = 56 * 1024 * 1024

SPAN = 128
ROW_TILE = 512
FFN_ROW_TILE = 1024
ATT_TILE = 512
FFN_CHUNK = 256
FFN_NCHUNK = FFN_DIM // FFN_CHUNK
CONV_HALO = 32
CONV_ROWS = 64
NEG_BIG = -1e30
LOG2_E = 1.4426950408889634
SLABS = GROUP_WIDTH // V7X_LANES
D_SLABS = D_MODEL // V7X_LANES

F32 = jnp.float32
BF16 = jnp.bfloat16


def _params(n_axes):
    return pltpu.CompilerParams(
        dimension_semantics=("arbitrary",) * n_axes,
        vmem_limit_bytes=V7X_VMEM_LIMIT_BYTES)


def _rms(x, g):
    return x * lax.rsqrt(jnp.mean(x * x, axis=-1, keepdims=True) + EPS) * g


def _sigmoid(x):
    return 1.0 / (1.0 + jnp.exp(-x))


def _slab(l):
    return slice(l * V7X_LANES, (l + 1) * V7X_LANES)


def _qkv_kernel(x_ref, pos_ref, g_ref, invf_ref, sgnm_ref, sgnp_ref, w_ref,
                o1_ref, o2_ref, o3_ref, hn_ref, perm_ref):
    tm = x_ref.shape[0]
    hn_ref[...] = _rms(x_ref[...], g_ref[...]).astype(BF16)
    ang = pos_ref[...].astype(F32) * invf_ref[...]
    cos = jnp.cos(ang)
    sin = jnp.sin(ang)
    sin_m = sin * sgnm_ref[...]
    sin_p = sin * sgnp_ref[...]
    qscale = HEAD_DIM ** -0.5 * LOG2_E
    outs = (o1_ref, o2_ref, o3_ref)
    for cb in range(QKV_WIDTH // GROUP_WIDTH):
        which, group = divmod(cb, N_GROUPS)
        dil = DILATED_GROUPS[group][1]
        o_ref = outs[group]
        res = jnp.dot(hn_ref[...], w_ref[:, cb * GROUP_WIDTH:(cb + 1) * GROUP_WIDTH],
                      preferred_element_type=F32)
        for l in range(SLABS):
            t = res[:, _slab(l)]
            if which < 2:
                t = (t * cos
                     + pltpu.roll(t, V7X_LANES - ROT_DIM // 2, 1) * sin_m
                     + pltpu.roll(t, ROT_DIM // 2, 1) * sin_p)
                if which == 0:
                    t = t * qscale
            cols = slice(which * GROUP_WIDTH + l * V7X_LANES,
                         which * GROUP_WIDTH + (l + 1) * V7X_LANES)
            if dil == 1:
                o_ref[0, :, cols] = t.astype(BF16)
            else:
                slot = (cb * SLABS + l) % perm_ref.shape[0]
                perm_ref[slot] = t
                for r in range(dil):
                    o_ref[r, :, cols] = perm_ref[slot, pl.ds(r, tm // dil, stride=dil), :].astype(BF16)


def _qkv_call(x, pos, g, invf, sgnm, sgnp, w):
    batch, seq, _ = x.shape
    tm = ROW_TILE
    row = lambda b, i: (b, i, 0)
    fixed = lambda b, i: (0, 0)
    out_shape, out_specs = [], []
    for _, dil in DILATED_GROUPS:
        out_shape.append(jax.ShapeDtypeStruct((batch, dil, seq // dil, 3 * GROUP_WIDTH), BF16))
        out_specs.append(pl.BlockSpec((None, dil, tm // dil, 3 * GROUP_WIDTH),
                                      lambda b, i: (b, 0, i, 0)))
    return pl.pallas_call(
        _qkv_kernel,
        out_shape=tuple(out_shape),
        grid=(batch, seq // tm),
        in_specs=[
            pl.BlockSpec((None, tm, D_MODEL), row),
            pl.BlockSpec((None, tm, 1), row),
            pl.BlockSpec((1, D_MODEL), fixed),
            pl.BlockSpec((1, V7X_LANES), fixed),
            pl.BlockSpec((1, V7X_LANES), fixed),
            pl.BlockSpec((1, V7X_LANES), fixed),
            pl.BlockSpec((D_MODEL, QKV_WIDTH), fixed),
        ],
        out_specs=tuple(out_specs),
        scratch_shapes=[pltpu.VMEM((tm, D_MODEL), BF16),
                        pltpu.VMEM((2 * SLABS, tm, V7X_LANES), F32)],
        compiler_params=_params(2),
        name="qkv_rope",
    )(x, pos, g, invf, sgnm, sgnp, w)


def _attn_kernel(q_ref, kc_ref, vc_ref, kp_ref, vp_ref, o_ref, lse_ref):
    jb = pl.program_id(2)
    lane = lax.broadcasted_iota(jnp.int32, (SPAN, V7X_LANES), 1)
    row = lax.broadcasted_iota(jnp.int32, (2 * SPAN, 2 * SPAN), 0) & (SPAN - 1)
    col = lax.broadcasted_iota(jnp.int32, (2 * SPAN, 2 * SPAN), 1)
    band = (col >= row) & (col <= row + SPAN)
    first = jnp.logical_and(band, jnp.logical_or(col >= SPAN, jb > 0))
    band_bias = jnp.where(band, 0.0, NEG_BIG)
    first_bias = jnp.where(first, 0.0, NEG_BIG)
    lo = lane < HEAD_DIM
    for i in range(ATT_TILE // SPAN):
        bias = first_bias if i == 0 else band_bias
        rows = slice(i * SPAN, (i + 1) * SPAN)
        lse_acc = jnp.zeros((SPAN, V7X_LANES), F32)
        for p in range(SLABS):
            cs = _slab(p)
            q2 = q_ref[rows, cs]
            if i == 0:
                kcat = jnp.concatenate([kp_ref[:, cs], kc_ref[0:SPAN, cs]], axis=0)
                vcat = jnp.concatenate([vp_ref[:, cs], vc_ref[0:SPAN, cs]], axis=0)
            else:
                kcat = kc_ref[(i - 1) * SPAN:(i + 1) * SPAN, cs]
                vcat = vc_ref[(i - 1) * SPAN:(i + 1) * SPAN, cs]
            zero = jnp.zeros_like(q2)
            qs = jnp.concatenate([jnp.where(lo, q2, zero), jnp.where(lo, zero, q2)],
                                 axis=0)
            s = lax.dot_general(qs, kcat, (((1,), (1,)), ((), ())),
                                preferred_element_type=F32)
            s = s + bias
            mx = jnp.max(s, axis=-1, keepdims=True)
            pr = jnp.exp2(s - mx)
            den = jnp.sum(pr, axis=-1, keepdims=True)
            pv = jnp.dot(pr.astype(BF16), vcat, preferred_element_type=F32)
            o_ref[rows, cs] = jnp.where(lo, pv[:SPAN], pv[SPAN:]).astype(BF16)
            lse_acc = jnp.where(lane == 2 * p, mx[:SPAN], lse_acc)
            lse_acc = jnp.where(lane == 2 * p + 1, mx[SPAN:], lse_acc)
            lse_acc = jnp.where(lane == N_SLOTS + 2 * p, den[:SPAN], lse_acc)
            lse_acc = jnp.where(lane == N_SLOTS + 2 * p + 1, den[SPAN:], lse_acc)
        lse_ref[rows, :] = lse_acc


def _attn_call(qkv_g):
    batch, dil, l, _ = qkv_g.shape
    sub = ATT_TILE // SPAN

    def spec(rows, which, prev):
        def imap(b, r, j):
            return (b, r, jnp.maximum(j * sub - 1, 0) if prev else j, which)
        return pl.BlockSpec((None, None, rows, GROUP_WIDTH), imap)

    here = lambda b, r, j: (b, r, j, 0)
    return pl.pallas_call(
        _attn_kernel,
        out_shape=(jax.ShapeDtypeStruct((batch, dil, l, GROUP_WIDTH), BF16),
                   jax.ShapeDtypeStruct((batch, dil, l, V7X_LANES), F32)),
        grid=(batch, dil, l // ATT_TILE),
        in_specs=[spec(ATT_TILE, 0, False), spec(ATT_TILE, 1, False),
                  spec(ATT_TILE, 2, False), spec(SPAN, 1, True), spec(SPAN, 2, True)],
        out_specs=(pl.BlockSpec((None, None, ATT_TILE, GROUP_WIDTH), here),
                   pl.BlockSpec((None, None, ATT_TILE, V7X_LANES), here)),
        compiler_params=_params(3),
        name=f"dilated_attn_d{dil}",
    )(qkv_g, qkv_g, qkv_g, qkv_g, qkv_g)


def _merge_kernel(o1_ref, o2_ref, o3_ref, l1_ref, l2_ref, l3_ref, x_ref, wo_ref,
                  g_ref, expand_ref, out_ref, os_ref, ls_ref):
    tm = x_ref.shape[0]
    o_tok, lses = [], []
    for gi, (o_ref, l_ref) in enumerate(((o1_ref, l1_ref), (o2_ref, l2_ref),
                                         (o3_ref, l3_ref))):
        dil = o_ref.shape[0]
        if dil == 1:
            o_tok.append(o_ref[0].astype(F32))
            lses.append(l_ref[0])
            continue
        for r in range(dil):
            rows = pl.ds(r, tm // dil, stride=dil)
            o_r = o_ref[r].astype(F32)
            for l in range(SLABS):
                os_ref[gi - 1, l, rows, :] = o_r[:, _slab(l)]
            ls_ref[gi - 1, rows, :] = l_ref[r]
        o_tok.append(jnp.concatenate([os_ref[gi - 1, l] for l in range(SLABS)], axis=1))
        lses.append(ls_ref[gi - 1])
    valid = lax.broadcasted_iota(jnp.int32, (tm, V7X_LANES), 1) < N_SLOTS
    mx = jnp.maximum(jnp.maximum(lses[0], lses[1]), lses[2])
    es = [jnp.exp2(l - mx) for l in lses]
    dens = [pltpu.roll(l, V7X_LANES - N_SLOTS, 1) for l in lses]
    total = es[0] * dens[0] + es[1] * dens[1] + es[2] * dens[2]
    rden = 1.0 / jnp.where(valid, total, 1.0)
    mixed = None
    for e, o in zip(es, o_tok):
        w = jnp.where(valid, e * rden, 0.0)
        hi = w.astype(BF16)
        lo = (w - hi.astype(F32)).astype(BF16)
        wb = (jnp.dot(hi, expand_ref[...], preferred_element_type=F32)
              + jnp.dot(lo, expand_ref[...], preferred_element_type=F32))
        term = wb * o
        mixed = term if mixed is None else mixed + term
    y = jnp.dot(mixed.astype(BF16), wo_ref[...], preferred_element_type=F32)
    out_ref[...] = x_ref[...] + _rms(y, g_ref[...])


def _merge_call(os, lses, x, wo, g, expand):
    batch, seq, _ = x.shape
    tm = ROW_TILE
    row = lambda b, i: (b, i, 0)
    fixed = lambda b, i: (0, 0)
    phase = lambda b, i: (b, 0, i, 0)
    o_specs = [pl.BlockSpec((None, d, tm // d, GROUP_WIDTH), phase) for _, d in DILATED_GROUPS]
    l_specs = [pl.BlockSpec((None, d, tm // d, V7X_LANES), phase) for _, d in DILATED_GROUPS]
    return pl.pallas_call(
        _merge_kernel,
        out_shape=jax.ShapeDtypeStruct((batch, seq, D_MODEL), F32),
        grid=(batch, seq // tm),
        in_specs=o_specs + l_specs
        + [pl.BlockSpec((None, tm, D_MODEL), row),
           pl.BlockSpec((GROUP_WIDTH, D_MODEL), fixed),
           pl.BlockSpec((1, D_MODEL), fixed),
           pl.BlockSpec((V7X_LANES, GROUP_WIDTH), fixed)],
        out_specs=pl.BlockSpec((None, tm, D_MODEL), row),
        scratch_shapes=[pltpu.VMEM((N_GROUPS - 1, SLABS, tm, V7X_LANES), F32),
                        pltpu.VMEM((N_GROUPS - 1, tm, V7X_LANES), F32)],
        compiler_params=_params(2),
        name="attn_merge_out",
    )(*os, *lses, x, wo, g, expand)


def _ffn_kernel(x_ref, g_in_ref, g_out_ref, wup_ref, wdw_ref, bdw_ref, wdn_ref,
                out_ref, hn_ref, s_ref, carry_ref, act_ref, *, tiles_per_seq):
    tm = x_ref.shape[0]
    pad = V7X_SUBLANES
    n_sl = 2 * FFN_CHUNK // V7X_LANES

    @pl.when(pl.program_id(0) % tiles_per_seq == 0)
    def _():
        carry_ref[...] = jnp.zeros_like(carry_ref)

    hn_ref[...] = _rms(x_ref[...], g_in_ref[...]).astype(BF16)

    for c in range(FFN_NCHUNK):
        starts = (c * FFN_CHUNK, FFN_DIM + c * FFN_CHUNK)
        ug = [jnp.dot(hn_ref[...], wup_ref[:, s0:s0 + FFN_CHUNK], preferred_element_type=F32)
              for s0 in starts]
        conv = []
        for l in range(n_sl):
            half, hl = divmod(l, n_sl // 2)
            lanes = slice(starts[half] + hl * V7X_LANES, starts[half] + (hl + 1) * V7X_LANES)
            sl = s_ref.at[(c % 2) * n_sl + l]
            cur = ug[half][:, _slab(hl)]
            sl[0:pad, :] = carry_ref[c, l]
            sl[pad:pad + tm, :] = cur
            carry_ref[c, l] = cur[tm - pad:tm, :]
            conv.append(cur * wdw_ref[2:3, lanes]
                        + sl[pl.ds(pad - 1, tm), :] * wdw_ref[1:2, lanes]
                        + sl[pl.ds(pad - 2, tm), :] * wdw_ref[0:1, lanes]
                        + bdw_ref[:, lanes])
        for l in range(n_sl // 2):
            gate = conv[n_sl // 2 + l]
            act = gate * _sigmoid(gate) * conv[l]
            act_ref[:, c * FFN_CHUNK + l * V7X_LANES:c * FFN_CHUNK + (l + 1) * V7X_LANES] = (
                act.astype(BF16))

    y = jnp.dot(act_ref[...], wdn_ref[...], preferred_element_type=F32)
    out_ref[...] = x_ref[...] + _rms(y, g_out_ref[...])


def _ffn_call(x, g_in, g_out, wup, wdw, bdw, wdn):
    batch, seq, _ = x.shape
    m = batch * seq
    tm = FFN_ROW_TILE
    n_sl = 2 * FFN_CHUNK // V7X_LANES
    row = lambda i: (i, 0)
    fixed2 = lambda i: (0, 0)
    out = pl.pallas_call(
        functools.partial(_ffn_kernel, tiles_per_seq=seq // tm),
        out_shape=jax.ShapeDtypeStruct((m, D_MODEL), F32),
        grid=(m // tm,),
        in_specs=[
            pl.BlockSpec((tm, D_MODEL), row),
            pl.BlockSpec((1, D_MODEL), fixed2),
            pl.BlockSpec((1, D_MODEL), fixed2),
            pl.BlockSpec((D_MODEL, 2 * FFN_DIM), fixed2, pipeline_mode=pl.Buffered(1)),
            pl.BlockSpec((FFN_CONV, 2 * FFN_DIM), fixed2),
            pl.BlockSpec((1, 2 * FFN_DIM), fixed2),
            pl.BlockSpec((FFN_DIM, D_MODEL), fixed2, pipeline_mode=pl.Buffered(1)),
        ],
        out_specs=pl.BlockSpec((tm, D_MODEL), row),
        scratch_shapes=[
            pltpu.VMEM((tm, D_MODEL), BF16),
            pltpu.VMEM((2 * n_sl, tm + V7X_SUBLANES, V7X_LANES), F32),
            pltpu.VMEM((FFN_NCHUNK, n_sl, V7X_SUBLANES, V7X_LANES), F32),
            pltpu.VMEM((tm, FFN_DIM), BF16),
        ],
        compiler_params=_params(1),
        name="conv_ffn",
    )(x.reshape(m, D_MODEL), g_in, g_out, wup, wdw, bdw, wdn)
    return out.reshape(batch, seq, D_MODEL)


def _conformer_kernel(x_ref, g_in_ref, g_out_ref, w1_ref, b1_ref, wdw_ref, bdw_ref,
                      lng_ref, lnb_ref, w2_ref, b2_ref, out_ref, hn_ref, s_ref,
                      *, tiles_per_seq):
    tm = x_ref.shape[0]
    halo = CONV_HALO

    @pl.when(pl.program_id(0) % tiles_per_seq == 0)
    def _():
        s_ref[:, 0:halo, :] = jnp.zeros((D_SLABS, halo, V7X_LANES), F32)

    x = x_ref[...]
    hn_ref[...] = _rms(x, g_in_ref[...]).astype(BF16)

    first = halo - (CONV_KERNEL - 1)
    ys = []
    per_dot = 2
    for lb in range(D_SLABS):
        if lb % per_dot == 0:
            c0 = lb * V7X_LANES
            c1 = c0 + per_dot * V7X_LANES
            a = (jnp.dot(hn_ref[...], w1_ref[:, c0:c1], preferred_element_type=F32)
                 + b1_ref[:, c0:c1])
            gate = (jnp.dot(hn_ref[...], w1_ref[:, D_MODEL + c0:D_MODEL + c1],
                            preferred_element_type=F32)
                    + b1_ref[:, D_MODEL + c0:D_MODEL + c1])
            u = a * _sigmoid(gate)
        s_ref[lb, halo:halo + tm, :] = u[:, _slab(lb % per_dot)]
        chunks = []
        for rc in range(tm // CONV_ROWS):
            acc = jnp.zeros((CONV_ROWS, V7X_LANES), F32)
            for j in range(CONV_KERNEL):
                acc = acc + (s_ref[lb, pl.ds(rc * CONV_ROWS + first + j, CONV_ROWS), :]
                             * wdw_ref[lb, j:j + 1, :])
            chunks.append(acc)
        ys.append(jnp.concatenate(chunks, axis=0))
        s_ref[lb, 0:halo, :] = s_ref[lb, tm:tm + halo, :]

    v = jnp.concatenate(ys, axis=1) + bdw_ref[...]
    mu = jnp.mean(v, axis=-1, keepdims=True)
    vc = v - mu
    var = jnp.mean(vc * vc, axis=-1, keepdims=True)
    ln = vc * lax.rsqrt(var + EPS) * lng_ref[...] + lnb_ref[...]
    act = (ln * _sigmoid(ln)).astype(BF16)
    y = jnp.dot(act, w2_ref[...], preferred_element_type=F32) + b2_ref[...]
    out_ref[...] = x + _rms(y, g_out_ref[...])


def _conformer_call(x, g_in, g_out, w1, b1, wdw, bdw, lng, lnb, w2, b2):
    batch, seq, _ = x.shape
    m = batch * seq
    tm = ROW_TILE
    row = lambda i: (i, 0)
    fixed = lambda i: (0, 0)
    vec = pl.BlockSpec((1, D_MODEL), fixed)
    out = pl.pallas_call(
        functools.partial(_conformer_kernel, tiles_per_seq=seq // tm),
        out_shape=jax.ShapeDtypeStruct((m, D_MODEL), F32),
        grid=(m // tm,),
        in_specs=[
            pl.BlockSpec((tm, D_MODEL), row), vec, vec,
            pl.BlockSpec((D_MODEL, 2 * D_MODEL), fixed),
            pl.BlockSpec((1, 2 * D_MODEL), fixed),
            pl.BlockSpec((D_SLABS, CONV_HALO, V7X_LANES), lambda i: (0, 0, 0)),
            vec, vec, vec,
            pl.BlockSpec((D_MODEL, D_MODEL), fixed),
            vec,
        ],
        out_specs=pl.BlockSpec((tm, D_MODEL), row),
        scratch_shapes=[
            pltpu.VMEM((tm, D_MODEL), BF16),
            pltpu.VMEM((D_SLABS, tm + CONV_HALO, V7X_LANES), F32),
        ],
        compiler_params=_params(1),
        name="conformer_conv",
    )(x.reshape(m, D_MODEL), g_in, g_out, w1, b1, wdw, bdw, lng, lnb, w2, b2)
    return out.reshape(batch, seq, D_MODEL)


def _rope_tables():
    half = ROT_DIM // 2
    inv_freq = ROPE_THETA ** (-jnp.arange(half, dtype=F32) / half)
    d = jnp.arange(V7X_LANES) % HEAD_DIM
    invf = jnp.where(d < ROT_DIM, jnp.tile(inv_freq, V7X_LANES // half), 0.0)
    sgnm = jnp.where(d < half, -1.0, 0.0)
    sgnp = jnp.where((d >= half) & (d < ROT_DIM), 1.0, 0.0)
    as_row = lambda a: a.astype(F32).reshape(1, V7X_LANES)
    return as_row(invf), as_row(sgnm), as_row(sgnp)


def _ffn_weights(w_up, w_dw, b_dw, w_down):
    return w_up.astype(BF16), w_dw, b_dw.reshape(1, -1), w_down.astype(BF16)


def kernel(x, positions, norm_g, attn_w_qkv, attn_w_o, conv_w_pw1, conv_b_pw1, conv_w_dw, conv_b_dw, conv_ln_g, conv_ln_b, conv_w_pw2, conv_b_pw2, ffn_w_up, ffn_w_dw, ffn_b_dw, ffn_w_down):
    batch, seq, _ = x.shape
    pos = positions.reshape(batch, seq, 1)
    vec = lambda a: a.reshape(1, -1)
    for window, dilation in DILATED_GROUPS:
        assert window // dilation == SPAN

    g = norm_g[0]
    invf, sgnm, sgnp = _rope_tables()
    qkv_groups = _qkv_call(x, pos, vec(g[0]), invf, sgnm, sgnp, attn_w_qkv[0].astype(BF16))
    attn = [_attn_call(qkv_g) for qkv_g in qkv_groups]
    slot = jnp.arange(V7X_LANES)[:, None]
    expand = (jnp.arange(GROUP_WIDTH)[None, :] // HEAD_DIM == slot).astype(BF16)
    x = _merge_call([o for o, _ in attn], [l for _, l in attn], x,
                    attn_w_o[0].astype(BF16), vec(g[1]), expand)
    x = _ffn_call(x, vec(g[2]), vec(g[3]),
                  *_ffn_weights(ffn_w_up[0], ffn_w_dw[0], ffn_b_dw[0], ffn_w_down[0]))

    g = norm_g[1]
    wdw = jnp.pad(conv_w_dw[0], ((0, CONV_HALO - CONV_KERNEL), (0, 0)))
    wdw = wdw.reshape(CONV_HALO, D_SLABS, V7X_LANES).transpose(1, 0, 2)
    x = _conformer_call(x, vec(g[0]), vec(g[1]),
                        conv_w_pw1[0].astype(BF16), vec(conv_b_pw1[0]),
                        wdw, vec(conv_b_dw[0]), vec(conv_ln_g[0]), vec(conv_ln_b[0]),
                        conv_w_pw2[0].astype(BF16), vec(conv_b_pw2[0]))
    x = _ffn_call(x, vec(g[2]), vec(g[3]),
                  *_ffn_weights(ffn_w_up[1], ffn_w_dw[1], ffn_b_dw[1], ffn_w_down[1]))
    return x
```
